```python
import math
import jax, jax.numpy as jnp
from jax import lax
import numpy as np

D_MODEL = 4096
BATCH = 8
SEQ = 2048
DEPTH = 2
DEC_BATCH = 4
DEC_SEQ = 4096
PAST_LEN = 128

N_META = 16
Q_BLOCK = 128
LN_EPS = 1e-5
RMS_EPS = 1e-6
DA_HEADS = 12
DA_QK_DIM = 64
DA_V_DIM = 128
DA_QK_COLS = 2 * DA_HEADS * DA_QK_DIM
DA_WIDTH = DA_HEADS * DA_V_DIM
MLA_HEADS = 12
MLA_Q_RANK = 1024
MLA_KV_RANK = 512
MLA_NOPE = 128
MLA_ROPE = 64
MLA_V = 128
MLA_WIDTH = MLA_HEADS * MLA_V
ROPE_THETA = 10000.0
HY_WIDTH = 1024
HY_SHORT = 3
HY_BANDS = 16
HY_EMB = 2 * HY_BANDS + 1
HY_FFN = 64
HY_DECAY_TARGET = 1e-2
HY_DECAY_SHORT = 0.3
HY_DECAY_LONG = 1.5
N_EXPERTS = 32
TOP_K = 4
D_FF = 512
SWIGLU_LIMIT = 7.0
SWIGLU_ALPHA = 1.702
MIX_WIDTH = DA_WIDTH + MLA_WIDTH + HY_WIDTH
IN_SIZES = (DA_QK_COLS, DA_QK_COLS, DA_WIDTH, MLA_Q_RANK, MLA_KV_RANK, MLA_ROPE, 3 * HY_WIDTH)
IN_WIDTH = sum(IN_SIZES)
IN_SPLITS = tuple(int(s) for s in np.cumsum(IN_SIZES)[:-1])
DEEPNORM_ALPHA = (2 * DEPTH) ** 0.25
DEEPNORM_BETA = (8 * DEPTH) ** -0.25

kernel_name = 'hybrid_parallel_group_encoder'


def layer_norm(x, g, b):
    xf = x.astype(jnp.float32)
    mu = jnp.mean(xf, axis=-1, keepdims=True)
    var = jnp.mean(jnp.square(xf - mu), axis=-1, keepdims=True)
    return ((xf - mu) * lax.rsqrt(var + LN_EPS) * g.astype(jnp.float32) + b.astype(jnp.float32)).astype(x.dtype)


def rms_norm(x, g):
    xf = x.astype(jnp.float32)
    inv = lax.rsqrt(jnp.mean(jnp.square(xf), axis=-1, keepdims=True) + RMS_EPS)
    return (xf * inv * g.astype(jnp.float32)).astype(x.dtype)


def rope(x, pos):
    half = x.shape[-1] // 2
    inv = ROPE_THETA ** (-jnp.arange(half, dtype=jnp.float32) / half)
    ang = pos.astype(jnp.float32)[:, None] * inv[None, :]
    cos, sin = jnp.cos(ang), jnp.sin(ang)
    xf = x.astype(jnp.float32)
    x1, x2 = xf[..., :half], xf[..., half:]
    return jnp.concatenate([x1 * cos - x2 * sin, x1 * sin + x2 * cos], axis=-1).astype(x.dtype)


def query_blocks(fn, qs, pos):
    head = fn(tuple(q[:, :, :N_META] for q in qs), pos[:N_META])
    n_blk = (pos.shape[0] - N_META) // Q_BLOCK

    def to_blocks(a):
        a = a[:, :, N_META:]
        a = a.reshape(a.shape[:2] + (n_blk, Q_BLOCK) + a.shape[3:])
        return jnp.moveaxis(a, 2, 0)

    body = lax.map(lambda args: fn(args[0], args[1]),
                   (tuple(to_blocks(q) for q in qs), pos[N_META:].reshape(n_blk, Q_BLOCK)))
    body = jnp.moveaxis(body, 0, 2)
    body = body.reshape(body.shape[:2] + (n_blk * Q_BLOCK, body.shape[-1]))
    return jnp.concatenate([head, body], axis=2)


def diff_attention(q, k, v, lam, slopes, pos):
    scale = DA_QK_DIM ** -0.5

    def block(qs, q_pos):
        (qb,) = qs
        s = jnp.einsum('bhqmd,bhkmd->bhmqk', qb, k).astype(jnp.float32) * scale
        dist = jnp.abs(q_pos[:, None] - pos[None, :]).astype(jnp.float32)
        s = s - slopes[:, None, None, None] * dist
        p = jax.nn.softmax(s, axis=-1)
        a = p[:, :, 0] - lam * p[:, :, 1]
        return jnp.einsum('bhqk,bhkv->bhqv', a.astype(v.dtype), v)

    return query_blocks(block, (q,), pos)


def mla_attention(q_nope, q_rope, k_nope, k_rope, v, pos):
    scale = (MLA_NOPE + MLA_ROPE) ** -0.5

    def block(qs, q_pos):
        qn, qr = qs
        s = jnp.einsum('bhqd,bhkd->bhqk', qn, k_nope) + jnp.einsum('bhqd,bkd->bhqk', qr, k_rope)
        p = jax.nn.softmax(s.astype(jnp.float32) * scale, axis=-1)
        return jnp.einsum('bhqk,bhkv->bhqv', p.astype(v.dtype), v)

    return query_blocks(block, (q_nope, q_rope), pos)


def hyena_filter_taps(L, w1, b1, w2, b2, w3, b3, w4, freq):
    f32 = jnp.float32
    t = jnp.linspace(0.0, 1.0, L, dtype=f32)[:, None]
    w = (2.0 * math.pi / L) * jnp.arange(L, dtype=f32)
    bands = jnp.linspace(1e-4, HY_BANDS - 1, HY_BANDS, dtype=f32)
    ang = w[:, None] * bands[None, :]
    z = jnp.concatenate([t, jnp.cos(ang), -jnp.sin(ang)], axis=-1)
    fr = freq.astype(f32)
    h = jnp.sin(fr * (z @ w1.astype(f32) + b1.astype(f32)))
    h = jnp.sin(fr * (h @ w2.astype(f32) + b2.astype(f32)))
    h = jnp.sin(fr * (h @ w3.astype(f32) + b3.astype(f32)))
    h = (h @ w4.astype(f32)).reshape(L, 2, HY_WIDTH)
    deltas = jnp.linspace(math.log(HY_DECAY_TARGET) / HY_DECAY_LONG, math.log(HY_DECAY_TARGET) / HY_DECAY_SHORT,
                          HY_WIDTH, dtype=f32)
    h = h * jnp.exp(-t * jnp.abs(deltas))[:, None, :]
    taps = jnp.concatenate([h[:, 0], jnp.zeros((1, HY_WIDTH), f32), h[:0:-1, 1]], axis=0)
    return taps / jnp.sum(jnp.abs(taps), axis=0, keepdims=True)


def hyena_mixer(u, short_w, short_b, d_skip, taps):
    L = u.shape[1]
    up = jnp.pad(u, ((0, 0), (1, 1), (0, 0)))
    uc = up[:, :-2] * short_w[0] + up[:, 1:-1] * short_w[1] + up[:, 2:] * short_w[2] + short_b
    x0, x1, v = jnp.split(uc, 3, axis=-1)
    z = (x1 * v).astype(jnp.float32)
    y = jnp.fft.irfft(jnp.fft.rfft(z, n=2 * L, axis=1) * jnp.fft.rfft(taps, axis=0)[None], n=2 * L, axis=1)[:, :L]
    y = y + z * d_skip.astype(jnp.float32)
    return (x0.astype(jnp.float32) * y).astype(u.dtype)


def token_mixer(h, p, lam_init):
    B, L, _ = h.shape
    pos = jnp.arange(L)
    qa, ka, va, cq, ckv, kr, hy = jnp.split(h @ p['w_in'], IN_SPLITS, axis=-1)

    qa = qa.reshape(B, L, DA_HEADS, 2, DA_QK_DIM).transpose(0, 2, 1, 3, 4)
    ka = ka.reshape(B, L, DA_HEADS, 2, DA_QK_DIM).transpose(0, 2, 1, 3, 4)
    va = va.reshape(B, L, DA_HEADS, DA_V_DIM).transpose(0, 2, 1, 3)
    lv = p['da_lambda'].astype(jnp.float32)
    lam = jnp.exp(jnp.sum(lv[0] * lv[1])) - jnp.exp(jnp.sum(lv[2] * lv[3])) + lam_init
    slopes = jnp.exp2(-8.0 * jnp.arange(1, DA_HEADS + 1, dtype=jnp.float32) / DA_HEADS)
    oa = diff_attention(qa, ka, va, lam, slopes, pos)
    oa = rms_norm(oa, p['da_norm_g']) * (1.0 - lam_init)
    oa = oa.transpose(0, 2, 1, 3).reshape(B, L, DA_WIDTH)

    q = (rms_norm(cq, p['mla_q_norm_g']) @ p['w_uq']).reshape(B, L, MLA_HEADS, MLA_NOPE + MLA_ROPE)
    q = q.transpose(0, 2, 1, 3)
    kv = (rms_norm(ckv, p['mla_kv_norm_g']) @ p['w_ukv']).reshape(B, L, MLA_HEADS, MLA_NOPE + MLA_V)
    kv = kv.transpose(0, 2, 1, 3)
    ob = mla_attention(q[..., :MLA_NOPE], rope(q[..., MLA_NOPE:], pos), kv[..., :MLA_NOPE],
                       rope(kr, pos), kv[..., MLA_NOPE:], pos)
    ob = rms_norm(ob.transpose(0, 2, 1, 3).reshape(B, L, MLA_WIDTH), p['mla_out_g'])

    taps = hyena_filter_taps(L, p['hy_ffn_w1'], p['hy_ffn_b1'], p['hy_ffn_w2'], p['hy_ffn_b2'],
                             p['hy_ffn_w3'], p['hy_ffn_b3'], p['hy_ffn_w4'], p['hy_freq'])
    oc = rms_norm(hyena_mixer(hy, p['hy_short_w'], p['hy_short_b'], p['hy_d'], taps), p['hy_out_g'])

    return jnp.concatenate([oa, ob, oc], axis=-1) @ p['w_o']


def moe(x, w_router, b_router, w_gate, b_gate, w_up, b_up, w_down, b_down):
    logits = (x @ w_router).astype(jnp.float32) + b_router.astype(jnp.float32)
    top_v, top_i = lax.top_k(logits, TOP_K)
    top_w = jax.nn.softmax(top_v, axis=-1)
    combine = jnp.einsum('tk,tke->te', top_w, jax.nn.one_hot(top_i, N_EXPERTS, dtype=jnp.float32))
    combine = combine.astype(x.dtype)
    g = jnp.minimum(jnp.einsum('td,edf->tef', x, w_gate) + b_gate, SWIGLU_LIMIT)
    u = jnp.clip(jnp.einsum('td,edf->tef', x, w_up) + b_up, -SWIGLU_LIMIT, SWIGLU_LIMIT)
    hid = (u + 1.0) * g * jax.nn.sigmoid(SWIGLU_ALPHA * g) * combine[:, :, None]
    return jnp.einsum('tef,efd->td', hid, w_down) + combine @ b_down


def setup_inputs(seed: int = 0) -> dict:
    key = jax.random.key(seed)
    ks = iter(jax.random.split(key, 48))
    f32 = jnp.float32
    D = D_MODEL

    def nrm(shape, scale):
        return jax.random.normal(next(ks), shape, f32) * scale

    def gain(shape):
        return 1.0 + nrm(shape, 0.02)

    return {
        'x_prompt': nrm((BATCH, SEQ, D), 1.0),
        'x_sample': nrm((DEC_BATCH, DEC_SEQ, D), 1.0),
        'meta_tokens': nrm((N_META, D), 1.0),
        'emb_ln_g': gain((D,)),
        'emb_ln_b': nrm((D,), 0.02),
        'w_in': nrm((DEPTH, D, IN_WIDTH), D ** -0.5),
        'w_o': nrm((DEPTH, MIX_WIDTH, D), DEEPNORM_BETA * MIX_WIDTH ** -0.5),
        'da_lambda': nrm((DEPTH, 4, DA_QK_DIM), 0.1),
        'da_norm_g': gain((DEPTH, DA_V_DIM)),
        'mla_q_norm_g': gain((DEPTH, MLA_Q_RANK)),
        'mla_kv_norm_g': gain((DEPTH, MLA_KV_RANK)),
        'w_uq': nrm((DEPTH, MLA_Q_RANK, MLA_HEADS * (MLA_NOPE + MLA_ROPE)), MLA_Q_RANK ** -0.5),
        'w_ukv': nrm((DEPTH, MLA_KV_RANK, MLA_HEADS * (MLA_NOPE + MLA_V)), MLA_KV_RANK ** -0.5),
        'mla_out_g': gain((DEPTH, MLA_WIDTH)),
        'hy_short_w': nrm((DEPTH, HY_SHORT, 3 * HY_WIDTH), HY_SHORT ** -0.5),
        'hy_short_b': nrm((DEPTH, 3 * HY_WIDTH), 0.02),
        'hy_ffn_w1': nrm((DEPTH, HY_EMB, HY_FFN), HY_EMB ** -0.5),
        'hy_ffn_b1': nrm((DEPTH, HY_FFN), 0.1),
        'hy_ffn_w2': nrm((DEPTH, HY_FFN, HY_FFN), HY_FFN ** -0.5),
        'hy_ffn_b2': nrm((DEPTH, HY_FFN), 0.1),
        'hy_ffn_w3': nrm((DEPTH, HY_FFN, HY_FFN), HY_FFN ** -0.5),
        'hy_ffn_b3': nrm((DEPTH, HY_FFN), 0.1),
        'hy_ffn_w4': nrm((DEPTH, HY_FFN, 2 * HY_WIDTH), HY_FFN ** -0.5),
        'hy_freq': 1.0 + nrm((DEPTH, HY_FFN), 0.1),
        'hy_d': nrm((DEPTH, HY_WIDTH), 1.0),
        'hy_out_g': gain((DEPTH, HY_WIDTH)),
        'ln1_g': gain((DEPTH, D)),
        'ln1_b': nrm((DEPTH, D), 0.02),
        'ln2_g': gain((DEPTH, D)),
        'ln2_b': nrm((DEPTH, D), 0.02),
        'w_router': nrm((DEPTH, D, N_EXPERTS), D ** -0.5),
        'b_router': nrm((DEPTH, N_EXPERTS), 0.01),
        'w_gate': nrm((DEPTH, N_EXPERTS, D, D_FF), D ** -0.5),
        'b_gate': nrm((DEPTH, N_EXPERTS, D_FF), 0.01),
        'w_up': nrm((DEPTH, N_EXPERTS, D, D_FF), D ** -0.5),
        'b_up': nrm((DEPTH, N_EXPERTS, D_FF), 0.01),
        'w_down': nrm((DEPTH, N_EXPERTS, D_FF, D), DEEPNORM_BETA * D_FF ** -0.5),
        'b_down': nrm((DEPTH, N_EXPERTS, D), 0.01),
    }


def reference(x_prompt, x_sample, meta_tokens, emb_ln_g, emb_ln_b, w_in, w_o, da_lambda, da_norm_g,
              mla_q_norm_g, mla_kv_norm_g, w_uq, w_ukv, mla_out_g, hy_short_w, hy_short_b,
              hy_ffn_w1, hy_ffn_b1, hy_ffn_w2, hy_ffn_b2, hy_ffn_w3, hy_ffn_b3, hy_ffn_w4, hy_freq,
              hy_d, hy_out_g, ln1_g, ln1_b, ln2_g, ln2_b, w_router, b_router,
              w_gate, b_gate, w_up, b_up, w_down, b_down):
    def add_meta(x):
        m = jnp.broadcast_to(meta_tokens[None], (x.shape[0], N_META, D_MODEL)).astype(x.dtype)
        return layer_norm(jnp.concatenate([m, x], axis=1), emb_ln_g, emb_ln_b)

    hp = add_meta(x_prompt)
    hs = add_meta(x_sample)
    n_prompt = hp.shape[0] * hp.shape[1]

    for l in range(DEPTH):
        lam_init = 0.8 - 0.6 * math.exp(-0.3 * l)
        p = dict(w_in=w_in[l], w_o=w_o[l], da_lambda=da_lambda[l], da_norm_g=da_norm_g[l],
                 mla_q_norm_g=mla_q_norm_g[l], mla_kv_norm_g=mla_kv_norm_g[l], w_uq=w_uq[l],
                 w_ukv=w_ukv[l], mla_out_g=mla_out_g[l], hy_short_w=hy_short_w[l],
                 hy_short_b=hy_short_b[l], hy_ffn_w1=hy_ffn_w1[l], hy_ffn_b1=hy_ffn_b1[l],
                 hy_ffn_w2=hy_ffn_w2[l], hy_ffn_b2=hy_ffn_b2[l], hy_ffn_w3=hy_ffn_w3[l],
                 hy_ffn_b3=hy_ffn_b3[l], hy_ffn_w4=hy_ffn_w4[l], hy_freq=hy_freq[l], hy_d=hy_d[l],
                 hy_out_g=hy_out_g[l])
        hp = layer_norm(DEEPNORM_ALPHA * hp + token_mixer(hp, p, lam_init), ln1_g[l], ln1_b[l])
        hs = layer_norm(DEEPNORM_ALPHA * hs + token_mixer(hs, p, lam_init), ln1_g[l], ln1_b[l])
        flat = jnp.concatenate([hp.reshape(-1, D_MODEL), hs.reshape(-1, D_MODEL)], axis=0)
        ffn = moe(flat, w_router[l], b_router[l], w_gate[l], b_gate[l], w_up[l], b_up[l], w_down[l], b_down[l])
        flat = layer_norm(DEEPNORM_ALPHA * flat + ffn, ln2_g[l], ln2_b[l])
        hp = flat[:n_prompt].reshape(hp.shape)
        hs = flat[n_prompt:].reshape(hs.shape)

    y_prompt = hp[:, N_META:]
    y_sample = hs[:, N_META:]
    return (y_prompt, y_sample)
```

```python
import functools
import math

import jax
import jax.numpy as jnp
from jax import lax
from jax.experimental import pallas as pl
from jax.experimental.pallas import tpu as pltpu

F32, BF16, I32, U32 = jnp.float32, jnp.bfloat16, jnp.int32, jnp.uint32

D_MODEL = 4096
DEPTH = 2
N_META = 16
LN_EPS = 1e-5
RMS_EPS = 1e-6
DA_HEADS = 12
DA_QK_DIM = 64
DA_V_DIM = 128
MLA_HEADS = 12
MLA_Q_RANK = 1024
MLA_KV_RANK = 512
MLA_NOPE = 128
MLA_ROPE = 64
MLA_V = 128
ROPE_THETA = 10000.0
HY_WIDTH = 1024
HY_BANDS = 16
HY_DECAY_TARGET = 1e-2
HY_DECAY_SHORT = 0.3
HY_DECAY_LONG = 1.5
N_EXPERTS = 32
TOP_K = 4
D_FF = 512
SWIGLU_LIMIT = 7.0
SWIGLU_ALPHA = 1.702

LANE = 128
VMEM_LIMIT = 56 * 1024 * 1024
MOE_BLOCK = 512
NEG_BIG = -1e30
FAR_POS = 1e9


def _cp(*sem):
    return pltpu.CompilerParams(dimension_semantics=sem, vmem_limit_bytes=VMEM_LIMIT)


def _tile(n, target, mult=16):
    best = None
    for d in range(mult, min(n, target) + 1, mult):
        if n % d == 0:
            best = d
    assert best is not None, (n, target, mult)
    return best


def _ln_math(x, g, b):
    mu = jnp.mean(x, axis=-1, keepdims=True)
    xc = x - mu
    var = jnp.mean(xc * xc, axis=-1, keepdims=True)
    return xc * lax.rsqrt(var + LN_EPS) * g + b


def _pack_pair(lo, hi):
    lo_b = lax.bitcast_convert_type(lo.astype(BF16).astype(F32), U32)
    hi_b = lax.bitcast_convert_type(hi.astype(BF16).astype(F32), U32)
    return (lo_b >> 16) | (hi_b & jnp.uint32(0xFFFF0000))


def _unpack_lo(w):
    return lax.bitcast_convert_type(w << 16, F32)


def _unpack_hi(w):
    return lax.bitcast_convert_type(w & jnp.uint32(0xFFFF0000), F32)


def _ln_embed_kernel(x_ref, g_ref, b_ref, of_ref, ob_ref):
    y = _ln_math(x_ref[...], g_ref[...], b_ref[...])
    of_ref[...] = y
    ob_ref[...] = y.astype(BF16)


def _ln_embed(x, g, b):
    R, D = x.shape
    tm = _tile(R, 256)
    row = pl.BlockSpec((tm, D), lambda i: (i, 0))
    vec = pl.BlockSpec((1, D), lambda i: (0, 0))
    return pl.pallas_call(
        _ln_embed_kernel, grid=(R // tm,), in_specs=[row, vec, vec], out_specs=[row, row],
        out_shape=[jax.ShapeDtypeStruct((R, D), F32), jax.ShapeDtypeStruct((R, D), BF16)],
        compiler_params=_cp("parallel"), name="ln_embed")(x, g.reshape(1, D), b.reshape(1, D))


def _ln_mix_kernel(h_ref, m_ref, g_ref, b_ref, of_ref, op_ref, *, alpha):
    y = _ln_math(alpha * h_ref[...] + m_ref[...], g_ref[...], b_ref[...])
    of_ref[...] = y
    half = y.shape[1] // 2
    op_ref[...] = _pack_pair(y[:, :half], y[:, half:])


def _ln_mix(h, mix, g, b, alpha):
    R, D = h.shape
    tm = _tile(R, 256)
    row = pl.BlockSpec((tm, D), lambda i: (i, 0))
    vec = pl.BlockSpec((1, D), lambda i: (0, 0))
    return pl.pallas_call(
        functools.partial(_ln_mix_kernel, alpha=alpha), grid=(R // tm,),
        in_specs=[row, row, vec, vec],
        out_specs=[row, pl.BlockSpec((tm, D // 2), lambda i: (i, 0))],
        out_shape=[jax.ShapeDtypeStruct((R, D), F32), jax.ShapeDtypeStruct((R, D // 2), U32)],
        compiler_params=_cp("parallel"), name="ln_mix")(h, mix, g.reshape(1, D), b.reshape(1, D))


def _mm_kernel(x_ref, w_ref, o_ref):
    o_ref[...] = jnp.dot(x_ref[...], w_ref[...], preferred_element_type=F32).astype(o_ref.dtype)


def _mm(x, w, *, tm, tn, out_dtype, k=None, x_kblock=0, name="mm"):
    M = x.shape[0]
    K, N = w.shape
    assert (k or x.shape[1]) == K and M % tm == 0 and N % tn == 0
    return pl.pallas_call(
        _mm_kernel, grid=(M // tm, N // tn),
        in_specs=[pl.BlockSpec((tm, K), lambda i, j: (i, x_kblock)),
                  pl.BlockSpec((K, tn), lambda i, j: (0, j))],
        out_specs=pl.BlockSpec((tm, tn), lambda i, j: (i, j)),
        out_shape=jax.ShapeDtypeStruct((M, N), out_dtype),
        compiler_params=_cp("parallel", "arbitrary"), name=name)(x, w)


def _rms_mm_kernel(x_ref, g_ref, w_ref, o_ref):
    x = x_ref[...].astype(F32)
    inv = lax.rsqrt(jnp.mean(x * x, axis=-1, keepdims=True) + RMS_EPS)
    xn = (x * inv * g_ref[...]).astype(BF16)
    o_ref[...] = jnp.dot(xn, w_ref[...], preferred_element_type=F32).astype(o_ref.dtype)


def _rms_mm(x, g, w, *, tm, tn, name):
    M = x.shape[0]
    K, N = w.shape
    assert M % tm == 0 and N % tn == 0
    return pl.pallas_call(
        _rms_mm_kernel, grid=(M // tm, N // tn),
        in_specs=[pl.BlockSpec((tm, K), lambda i, j: (i, 0)),
                  pl.BlockSpec((1, K), lambda i, j: (0, 0)),
                  pl.BlockSpec((K, tn), lambda i, j: (0, j))],
        out_specs=pl.BlockSpec((tm, tn), lambda i, j: (i, j)),
        out_shape=jax.ShapeDtypeStruct((M, N), BF16),
        compiler_params=_cp("parallel", "arbitrary"), name=name)(x, g.reshape(1, K), w)


def _wo_kernel(oa_ref, ob_ref, oc_ref, gb_ref, w1_ref, w2_ref, w3_ref, o_ref):
    ob = ob_ref[...].astype(F32)
    inv = lax.rsqrt(jnp.mean(ob * ob, axis=-1, keepdims=True) + RMS_EPS)
    obn = (ob * inv * gb_ref[...]).astype(BF16)
    acc = jnp.dot(oa_ref[...], w1_ref[...], preferred_element_type=F32)
    acc += jnp.dot(obn, w2_ref[...], preferred_element_type=F32)
    acc += jnp.dot(oc_ref[...], w3_ref[...], preferred_element_type=F32)
    o_ref[...] = acc


def _wo(oa, ob, oc, gb, w1, w2, w3):
    R = oa.shape[0]
    N = w1.shape[1]
    tm, tn = _tile(R, 512), _tile(N, 1024, LANE)
    xs = lambda a: pl.BlockSpec((tm, a.shape[1]), lambda j, i: (i, 0))
    ws = lambda a: pl.BlockSpec((a.shape[0], tn), lambda j, i: (0, j))
    return pl.pallas_call(
        _wo_kernel, grid=(N // tn, R // tm),
        in_specs=[xs(oa), xs(ob), xs(oc), pl.BlockSpec((1, ob.shape[1]), lambda j, i: (0, 0)),
                  ws(w1), ws(w2), ws(w3)],
        out_specs=pl.BlockSpec((tm, tn), lambda j, i: (i, j)),
        out_shape=jax.ShapeDtypeStruct((R, N), F32),
        compiler_params=_cp("parallel", "arbitrary"), name="w_o")(
            oa, ob, oc, gb.reshape(1, -1), w1, w2, w3)


def _da_kernel(sc_ref, q_ref, k_ref, v_ref, g_ref, o_ref, *, L, n_heads):
    h = pl.program_id(1)
    i = pl.program_id(2)
    tq, Lp = q_ref.shape[0], k_ref.shape[0]
    slope, lam = sc_ref[h], sc_ref[n_heads]
    q = q_ref[...]
    lane = lax.broadcasted_iota(I32, q.shape, 1)
    zero = jnp.zeros_like(q)
    q0 = jnp.where(lane < DA_QK_DIM, q, zero)
    q1 = jnp.where(lane >= DA_QK_DIM, q, zero)
    k = k_ref[...]
    dn = (((1,), (1,)), ((), ()))
    s0 = lax.dot_general(q0, k, dn, preferred_element_type=F32)
    s1 = lax.dot_general(q1, k, dn, preferred_element_type=F32)
    qpos = (i * tq + lax.broadcasted_iota(I32, (tq, 1), 0)).astype(F32)
    col = lax.broadcasted_iota(I32, (1, Lp), 1)
    kpos = jnp.where(col < L, col.astype(F32), FAR_POS)
    bias = slope * jnp.abs(qpos - kpos)
    s0 = s0 - bias
    s1 = s1 - bias
    p0 = jnp.exp(s0 - jnp.max(s0, axis=-1, keepdims=True))
    p1 = jnp.exp(s1 - jnp.max(s1, axis=-1, keepdims=True))
    l0 = jnp.sum(p0, axis=-1, keepdims=True)
    l1 = jnp.sum(p1, axis=-1, keepdims=True)
    a = (p0 - (lam * l0 / l1) * p1).astype(BF16)
    o = jnp.dot(a, v_ref[...], preferred_element_type=F32) / l0
    o = o * lax.rsqrt(jnp.mean(o * o, axis=-1, keepdims=True) + RMS_EPS) * g_ref[...]
    o_ref[...] = o.astype(BF16)


def _da_attention(wide, scal, gain, *, B, Lp, L, col0):
    H = DA_HEADS
    tq = _tile(Lp, max(16, (640 * 1024) // Lp))
    nq = Lp // tq
    c0 = col0 // LANE
    return pl.pallas_call(
        functools.partial(_da_kernel, L=L, n_heads=H), grid=(B, H, nq),
        in_specs=[pl.BlockSpec(memory_space=pltpu.SMEM),
                  pl.BlockSpec((tq, LANE), lambda b, h, i: (b * nq + i, c0 + h)),
                  pl.BlockSpec((Lp, LANE), lambda b, h, i: (b, c0 + H + h)),
                  pl.BlockSpec((Lp, LANE), lambda b, h, i: (b, c0 + 2 * H + h)),
                  pl.BlockSpec((1, LANE), lambda b, h, i: (0, 0))],
        out_specs=pl.BlockSpec((tq, LANE), lambda b, h, i: (b * nq + i, h)),
        out_shape=jax.ShapeDtypeStruct((B * Lp, H * LANE), BF16),
        compiler_params=_cp("parallel", "parallel", "arbitrary"), name="diff_attn")(
            scal, wide, wide, wide, gain.reshape(1, LANE))


def _rope_rot(x, c, s_up, s_dn):
    half = MLA_ROPE // 2
    return x * c + pltpu.roll(x, half, 1) * s_up + pltpu.roll(x, LANE - half, 1) * s_dn


def _mla_kernel(qn_ref, qr_ref, kn_ref, kr_ref, v_ref, cq_ref, uq_ref, dq_ref,
                ck_ref, uk_ref, dk_ref, o_ref, kcat_ref, *, L, scale):
    Lp = kn_ref.shape[0]

    @pl.when(pl.program_id(2) == 0)
    def _():
        kr = _rope_rot(kr_ref[...].astype(F32), ck_ref[...], uk_ref[...], dk_ref[...])
        kcat_ref[:, :LANE] = kn_ref[...]
        kcat_ref[:, LANE:] = kr.astype(BF16)

    qr = _rope_rot(qr_ref[...].astype(F32), cq_ref[...], uq_ref[...], dq_ref[...]) * scale
    qn = qn_ref[...].astype(F32) * scale
    q = jnp.concatenate([qn.astype(BF16), qr.astype(BF16)], axis=1)
    s = lax.dot_general(q, kcat_ref[...], (((1,), (1,)), ((), ())), preferred_element_type=F32)
    col = lax.broadcasted_iota(I32, (1, Lp), 1)
    s = s + jnp.where(col < L, 0.0, NEG_BIG)
    p = jnp.exp(s - jnp.max(s, axis=-1, keepdims=True))
    l = jnp.sum(p, axis=-1, keepdims=True)
    o = jnp.dot(p.astype(BF16), v_ref[...], preferred_element_type=F32) / l
    o_ref[...] = o.astype(BF16)


def _mla_attention(q, kv, wide2, tabs, *, B, Lp, L, kr_col):
    H = MLA_HEADS
    tq = _tile(Lp, max(16, (1024 * 1024) // Lp))
    nq = Lp // tq
    krb = kr_col // LANE
    qtab = pl.BlockSpec((tq, LANE), lambda b, h, i: (i, 0))
    ktab = pl.BlockSpec((Lp, LANE), lambda b, h, i: (0, 0))
    return pl.pallas_call(
        functools.partial(_mla_kernel, L=L, scale=(MLA_NOPE + MLA_ROPE) ** -0.5), grid=(B, H, nq),
        in_specs=[pl.BlockSpec((tq, LANE), lambda b, h, i: (b * nq + i, h)),
                  pl.BlockSpec((tq, LANE), lambda b, h, i: (b * nq + i, H + h)),
                  pl.BlockSpec((Lp, LANE), lambda b, h, i: (b, h)),
                  pl.BlockSpec((Lp, LANE), lambda b, h, i: (b, krb)),
                  pl.BlockSpec((Lp, LANE), lambda b, h, i: (b, H + h)),
                  qtab, qtab, qtab, ktab, ktab, ktab],
        out_specs=pl.BlockSpec((tq, LANE), lambda b, h, i: (b * nq + i, h)),
        out_shape=jax.ShapeDtypeStruct((B * Lp, H * LANE), BF16),
        scratch_shapes=[pltpu.VMEM((Lp, 2 * LANE), BF16)],
        compiler_params=_cp("parallel", "parallel", "arbitrary"), name="latent_attn")(
            q, q, kv, wide2, kv, *tabs, *tabs)


def _rope_tables(Lp):
    half = MLA_ROPE // 2
    inv = ROPE_THETA ** (-jnp.arange(half, dtype=F32) / half)
    ang = jnp.arange(Lp, dtype=F32)[:, None] * inv[None, :]
    cos, sin, z = jnp.cos(ang), jnp.sin(ang), jnp.zeros((Lp, half), F32)
    return (jnp.concatenate([cos, cos, z, z], axis=1),
            jnp.concatenate([z, sin, z, z], axis=1),
            jnp.concatenate([-sin, z, z, z], axis=1))


def _hy_pre_kernel(x0_ref, x1_ref, v_ref, w0_ref, w1_ref, wv_ref, b0_ref, b1_ref, bv_ref,
                   z_ref, x0c_ref, *, L):
    Lp = x0_ref.shape[0]
    row = lax.broadcasted_iota(I32, (Lp, 1), 0)
    valid = row < L

    def short_conv(u_ref, w_ref, b_ref):
        u = jnp.where(valid, u_ref[...].astype(F32), 0.0)
        w = w_ref[...]
        return (pltpu.roll(u, 1, 0) * w[0:1] + u * w[1:2] + pltpu.roll(u, Lp - 1, 0) * w[2:3]
                + b_ref[...])

    x0c_ref[...] = short_conv(x0_ref, w0_ref, b0_ref).astype(BF16)
    z = short_conv(x1_ref, w1_ref, b1_ref) * short_conv(v_ref, wv_ref, bv_ref)
    z_ref[...] = jnp.where(valid, z, 0.0).astype(BF16)


def _hy_pre(wide, short_w, short_b, *, B, Lp, L, col0):
    assert Lp > L
    nc = HY_WIDTH // LANE
    c0 = col0 // LANE
    u = lambda part: pl.BlockSpec((Lp, LANE), lambda b, c: (b, c0 + part * nc + c))
    w = lambda part: pl.BlockSpec((3, LANE), lambda b, c: (0, part * nc + c))
    bb = lambda part: pl.BlockSpec((1, LANE), lambda b, c: (0, part * nc + c))
    out = pl.BlockSpec((Lp, LANE), lambda b, c: (b, c))
    shp = jax.ShapeDtypeStruct((B * Lp, HY_WIDTH), BF16)
    sb = short_b.reshape(1, -1)
    return pl.pallas_call(
        functools.partial(_hy_pre_kernel, L=L), grid=(B, nc),
        in_specs=[u(0), u(1), u(2), w(0), w(1), w(2), bb(0), bb(1), bb(2)],
        out_specs=[out, out], out_shape=[shp, shp],
        compiler_params=_cp("parallel", "parallel"), name="hyena_gate")(
            wide, wide, wide, short_w, short_w, short_w, sb, sb, sb)


def _hy_fwd_kernel(fc_ref, fs_ref, z_ref, kr_ref, ki_ref, yr_ref, yi_ref):
    z = z_ref[...]
    zr = jnp.dot(fc_ref[...], z, preferred_element_type=F32)
    zi = jnp.dot(fs_ref[...], z, preferred_element_type=F32)
    kr, ki = kr_ref[...], ki_ref[...]
    yr_ref[...] = (zr * kr - zi * ki).astype(BF16)
    yi_ref[...] = (zr * ki + zi * kr).astype(BF16)


def _hy_fwd(fcat, z, kr, ki, *, B, Lp):
    C = z.shape[1]
    tf, tn = _tile(Lp, 576), _tile(C, 512, LANE)
    nf = Lp // tf
    fspec = lambda half: pl.BlockSpec((tf, Lp), lambda i, b, j: (i, half))
    kspec = pl.BlockSpec((tf, tn), lambda i, b, j: (i, j))
    out = pl.BlockSpec((tf, tn), lambda i, b, j: (b * nf + i, j))
    shp = jax.ShapeDtypeStruct((B * Lp, C), BF16)
    return pl.pallas_call(
        _hy_fwd_kernel, grid=(nf, B, C // tn),
        in_specs=[fspec(0), fspec(1), pl.BlockSpec((Lp, tn), lambda i, b, j: (b, j)), kspec, kspec],
        out_specs=[out, out], out_shape=[shp, shp],
        compiler_params=_cp("parallel", "parallel", "arbitrary"), name="hyena_dft")(
            fcat, fcat, z, kr, ki)


def _hy_inv_kernel(fc_ref, fs_ref, yr_ref, yi_ref, o_ref):
    acc = jnp.dot(fc_ref[...], yr_ref[...], preferred_element_type=F32)
    acc += jnp.dot(fs_ref[...], yi_ref[...], preferred_element_type=F32)
    o_ref[...] = acc


def _hy_inv(fcat, yr, yi, *, B, Lp):
    C = yr.shape[1]
    tf, tn = _tile(Lp, 576), _tile(C, 512, LANE)
    nf = Lp // tf
    fspec = lambda half: pl.BlockSpec((tf, Lp), lambda i, b, j: (i, half))
    yspec = pl.BlockSpec((Lp, tn), lambda i, b, j: (b, j))
    return pl.pallas_call(
        _hy_inv_kernel, grid=(nf, B, C // tn),
        in_specs=[fspec(0), fspec(1), yspec, yspec],
        out_specs=pl.BlockSpec((tf, tn), lambda i, b, j: (b * nf + i, j)),
        out_shape=jax.ShapeDtypeStruct((B * Lp, C), F32),
        compiler_params=_cp("parallel", "parallel", "arbitrary"), name="hyena_idft")(
            fcat, fcat, yr, yi)


def _hy_post_kernel(y_ref, z_ref, x0_ref, d_ref, g_ref, o_ref):
    y = y_ref[...] + z_ref[...].astype(F32) * d_ref[...]
    o = x0_ref[...].astype(F32) * y
    o = o * lax.rsqrt(jnp.mean(o * o, axis=-1, keepdims=True) + RMS_EPS) * g_ref[...]
    o_ref[...] = o.astype(BF16)


def _hy_post(y, z, x0c, d, g):
    R, C = y.shape
    tm = _tile(R, 512)
    row = pl.BlockSpec((tm, C), lambda i: (i, 0))
    vec = pl.BlockSpec((1, C), lambda i: (0, 0))
    return pl.pallas_call(
        _hy_post_kernel, grid=(R // tm,), in_specs=[row, row, row, vec, vec], out_specs=row,
        out_shape=jax.ShapeDtypeStruct((R, C), BF16),
        compiler_params=_cp("parallel"), name="hyena_out")(y, z, x0c, d.reshape(1, C), g.reshape(1, C))


def _dft_table(Lp):
    N = 2 * Lp - 1
    blk = 64
    f = jnp.arange(Lp, dtype=I32)[:, None]
    th = 2.0 * math.pi / N
    a0 = ((f * jnp.arange(0, Lp, blk, dtype=I32)[None, :]) % N).astype(F32) * th
    a1 = ((f * jnp.arange(blk, dtype=I32)[None, :]) % N).astype(F32) * th
    c0, s0, c1, s1 = jnp.cos(a0)[:, :, None], jnp.sin(a0)[:, :, None], jnp.cos(a1)[:, None, :], jnp.sin(a1)[:, None, :]
    fc = (c0 * c1 - s0 * s1).reshape(Lp, Lp)
    fs = (s0 * c1 + c0 * s1).reshape(Lp, Lp)
    return jnp.concatenate([fc, -fs], axis=1).astype(BF16)


def _hyena_taps(L, Lp, w1, b1, w2, b2, w3, b3, w4, freq):
    hp = lax.Precision.HIGHEST
    t = jnp.linspace(0.0, 1.0, L, dtype=F32)[:, None]
    w = (2.0 * math.pi / L) * jnp.arange(L, dtype=F32)
    bands = jnp.linspace(1e-4, HY_BANDS - 1, HY_BANDS, dtype=F32)
    ang = w[:, None] * bands[None, :]
    z = jnp.concatenate([t, jnp.cos(ang), -jnp.sin(ang)], axis=-1)
    h = jnp.sin(freq * (jnp.dot(z, w1, precision=hp) + b1))
    h = jnp.sin(freq * (jnp.dot(h, w2, precision=hp) + b2))
    h = jnp.sin(freq * (jnp.dot(h, w3, precision=hp) + b3))
    h = jnp.dot(h, w4, precision=hp).reshape(L, 2, HY_WIDTH)
    deltas = jnp.linspace(math.log(HY_DECAY_TARGET) / HY_DECAY_LONG, math.log(HY_DECAY_TARGET) / HY_DECAY_SHORT,
                          HY_WIDTH, dtype=F32)
    h = h * jnp.exp(-t * jnp.abs(deltas))[:, None, :]
    hf = h[:, 0]
    hb = h[:, 1].at[0].set(0.0)
    norm = jnp.sum(jnp.abs(hf), axis=0, keepdims=True) + jnp.sum(jnp.abs(hb), axis=0, keepdims=True)
    pad = ((0, Lp - L), (0, 0))
    return jnp.pad(hf / norm, pad), jnp.pad(hb / norm, pad)


def _hyena_spectrum(fcat, hf, hb, Lp):
    N = 2 * Lp - 1
    tm, tn = _tile(Lp, 576), _tile(HY_WIDTH, 512, LANE)
    kr = _mm(fcat, (hf + hb).astype(BF16), tm=tm, tn=tn, out_dtype=F32, k=Lp, x_kblock=0, name="hyena_spec_re")
    ki = _mm(fcat, (hf - hb).astype(BF16), tm=tm, tn=tn, out_dtype=F32, k=Lp, x_kblock=1, name="hyena_spec_im")
    wf = jnp.where(jnp.arange(Lp)[:, None] == 0, 1.0 / N, 2.0 / N).astype(F32)
    return kr * wf, ki * wf


def _router_kernel(x_ref, wh_ref, wl_ref, b_ref, info_ref, cnt_ref, carry_ref):
    tm = x_ref.shape[0]

    @pl.when(pl.program_id(0) == 0)
    def _():
        carry_ref[...] = jnp.zeros_like(carry_ref)

    x = x_ref[...]
    xh = x.astype(BF16)
    xl = (x - xh.astype(F32)).astype(BF16)
    wh, wl = wh_ref[...], wl_ref[...]
    logits = (jnp.dot(xh, wh, preferred_element_type=F32) + jnp.dot(xl, wh, preferred_element_type=F32)
              + jnp.dot(xh, wl, preferred_element_type=F32)) + b_ref[...]
    lane = lax.broadcasted_iota(I32, (tm, LANE), 1)
    vals, hots = [], []
    cur = logits
    for _ in range(TOP_K):
        m = jnp.max(cur, axis=-1, keepdims=True)
        idx = jnp.min(jnp.where(cur == m, lane, LANE), axis=-1, keepdims=True)
        hot = lane == idx
        vals.append(m)
        hots.append(hot)
        cur = jnp.where(hot, NEG_BIG, cur)
    es = [jnp.exp(v - vals[0]) for v in vals]
    den = es[0] + es[1] + es[2] + es[3]
    multihot = (hots[0] | hots[1] | hots[2] | hots[3]).astype(F32)
    r_i = lax.broadcasted_iota(I32, (tm, tm), 0)
    c_i = lax.broadcasted_iota(I32, (tm, tm), 1)
    tri = (c_i < r_i).astype(BF16)
    before = jnp.dot(tri, multihot.astype(BF16), preferred_element_type=F32) + carry_ref[...]
    info = jnp.zeros((tm, LANE), F32)
    for kk in range(TOP_K):
        e_k = jnp.sum(jnp.where(hots[kk], lane, 0), axis=-1, keepdims=True).astype(F32)
        r_k = jnp.sum(jnp.where(hots[kk], before, 0.0), axis=-1, keepdims=True)
        info = jnp.where(lane == kk, e_k, info)
        info = jnp.where(lane == TOP_K + kk, es[kk] / den, info)
        info = jnp.where(lane == 2 * TOP_K + kk, r_k, info)
    info_ref[...] = info
    carry_ref[...] += jnp.sum(multihot, axis=0, keepdims=True)
    cnt_ref[...] = carry_ref[...]


def _router(x, wh, wl, b):
    R, D = x.shape
    tm = _tile(R, 512)
    return pl.pallas_call(
        _router_kernel, grid=(R // tm,),
        in_specs=[pl.BlockSpec((tm, D), lambda i: (i, 0)), pl.BlockSpec((D, LANE), lambda i: (0, 0)),
                  pl.BlockSpec((D, LANE), lambda i: (0, 0)), pl.BlockSpec((1, LANE), lambda i: (0, 0))],
        out_specs=[pl.BlockSpec((tm, LANE), lambda i: (i, 0)), pl.BlockSpec((1, LANE), lambda i: (0, 0))],
        out_shape=[jax.ShapeDtypeStruct((R, LANE), F32), jax.ShapeDtypeStruct((1, LANE), F32)],
        scratch_shapes=[pltpu.VMEM((1, LANE), F32)],
        compiler_params=_cp("arbitrary"), name="router")(x, wh, wl, b)


def _row_copy(src_hbm, dst_ref, src_row, dst_row, sem):
    return pltpu.make_async_copy(src_hbm.at[pl.ds(src_row, 1)], dst_ref.at[pl.ds(dst_row, 1)], sem)


def _dispatch_kernel(nb_ref, tok_ref, x_hbm, o_hbm, sem):
    j = pl.program_id(0)
    G = tok_ref.shape[2]

    @pl.when(j < nb_ref[0])
    def _():
        def start(r, c):
            _row_copy(x_hbm, o_hbm, tok_ref[0, 0, r], j * G + r, sem).start()
            return c

        def wait(r, c):
            _row_copy(x_hbm, o_hbm, 0, j * G + r, sem).wait()
            return c

        lax.fori_loop(0, G, start, 0)
        lax.fori_loop(0, G, wait, 0)

    @pl.when(j >= nb_ref[0])
    def _():
        fill = pltpu.make_async_copy(x_hbm.at[pl.ds(0, G)], o_hbm.at[pl.ds(j * G, G)], sem)
        fill.start()
        fill.wait()


def _dispatch(xp, tok_of_slot, nb, *, n_blocks):
    G = MOE_BLOCK
    W = xp.shape[1]
    return pl.pallas_call(
        _dispatch_kernel,
        grid_spec=pltpu.PrefetchScalarGridSpec(
            num_scalar_prefetch=1, grid=(n_blocks,),
            in_specs=[pl.BlockSpec((1, 1, G), lambda j, nb: (j, 0, 0), memory_space=pltpu.SMEM),
                      pl.BlockSpec(memory_space=pl.ANY)],
            out_specs=pl.BlockSpec(memory_space=pl.ANY),
            scratch_shapes=[pltpu.SemaphoreType.DMA(())]),
        out_shape=jax.ShapeDtypeStruct((n_blocks * G, W), U32),
        compiler_params=_cp("arbitrary"), name="moe_dispatch")(
            nb, tok_of_slot.reshape(n_blocks, 1, G), xp)


def _experts_kernel(be_ref, nb_ref, x_ref, wg_ref, bg_ref, wu_ref, bu_ref, wd_ref, bd_ref, o_ref):
    j = pl.program_id(0)
    half = x_ref.shape[1]

    @pl.when(j < nb_ref[0])
    def _():
        xw = x_ref[...]
        xlo, xhi = _unpack_lo(xw).astype(BF16), _unpack_hi(xw).astype(BF16)

        def proj(w_ref, b_ref):
            return (jnp.dot(xlo, w_ref[0, :half], preferred_element_type=F32)
                    + jnp.dot(xhi, w_ref[0, half:], preferred_element_type=F32) + b_ref[0])

        g = jnp.minimum(proj(wg_ref, bg_ref), SWIGLU_LIMIT)
        u = jnp.clip(proj(wu_ref, bu_ref), -SWIGLU_LIMIT, SWIGLU_LIMIT)
        hid = ((u + 1.0) * g * jax.nn.sigmoid(SWIGLU_ALPHA * g)).astype(BF16)
        o = jnp.dot(hid, wd_ref[0], preferred_element_type=F32) + bd_ref[0]
        o_ref[...] = _pack_pair(o[:, :half], o[:, half:])

    @pl.when(j >= nb_ref[0])
    def _():
        o_ref[...] = jnp.zeros_like(o_ref)


def _experts(block_expert, nb, xs, wg, bg, wu, bu, wd, bd, *, n_blocks):
    G = MOE_BLOCK
    E, D, F = wg.shape
    W = xs.shape[1]
    act = lambda j, be, nb: (jnp.minimum(j, nb[0] - 1), 0)
    wsp = lambda shape: pl.BlockSpec((1,) + shape, lambda j, be, nb: (be[j], 0, 0))
    return pl.pallas_call(
        _experts_kernel,
        grid_spec=pltpu.PrefetchScalarGridSpec(
            num_scalar_prefetch=2, grid=(n_blocks,),
            in_specs=[pl.BlockSpec((G, W), act), wsp((D, F)), wsp((1, F)), wsp((D, F)), wsp((1, F)),
                      wsp((F, D)), wsp((1, D))],
            out_specs=pl.BlockSpec((G, W), lambda j, be, nb: (j, 0))),
        out_shape=jax.ShapeDtypeStruct((n_blocks * G, W), U32),
        compiler_params=_cp("arbitrary"), name="moe_experts")(
            block_expert, nb, xs, wg, bg.reshape(E, 1, F), wu, bu.reshape(E, 1, F), wd, bd.reshape(E, 1, D))


def _combine_kernel(slot_ref, w_ref, h_ref, g_ref, b_ref, os_hbm, of_ref, ob_ref, buf_ref, sem, *, alpha):
    tm = h_ref.shape[0]

    def start(r, c):
        for kk in range(TOP_K):
            _row_copy(os_hbm, buf_ref.at[kk], slot_ref[0, 0, r * TOP_K + kk], r, sem).start()
        return c

    def wait(r, c):
        for kk in range(TOP_K):
            _row_copy(os_hbm, buf_ref.at[kk], 0, r, sem).wait()
        return c

    lax.fori_loop(0, tm, start, 0)
    lax.fori_loop(0, tm, wait, 0)
    w = w_ref[...]
    lo = jnp.zeros((tm, buf_ref.shape[2]), F32)
    hi = jnp.zeros((tm, buf_ref.shape[2]), F32)
    for kk in range(TOP_K):
        word = buf_ref[kk]
        wk = w[:, TOP_K + kk:TOP_K + kk + 1]
        lo += wk * _unpack_lo(word)
        hi += wk * _unpack_hi(word)
    y = _ln_math(alpha * h_ref[...] + jnp.concatenate([lo, hi], axis=1), g_ref[...], b_ref[...])
    of_ref[...] = y
    ob_ref[...] = y.astype(BF16)


def _combine(slot, info, h, g, b, os_, alpha):
    R, D = h.shape
    tm = _tile(R, 128)
    nblk = R // tm
    row = pl.BlockSpec((tm, D), lambda i: (i, 0))
    vec = pl.BlockSpec((1, D), lambda i: (0, 0))
    return pl.pallas_call(
        functools.partial(_combine_kernel, alpha=alpha), grid=(nblk,),
        in_specs=[pl.BlockSpec((1, 1, tm * TOP_K), lambda i: (i, 0, 0), memory_space=pltpu.SMEM),
                  pl.BlockSpec((tm, LANE), lambda i: (i, 0)), row, vec, vec,
                  pl.BlockSpec(memory_space=pl.ANY)],
        out_specs=[row, row],
        out_shape=[jax.ShapeDtypeStruct((R, D), F32), jax.ShapeDtypeStruct((R, D), BF16)],
        scratch_shapes=[pltpu.VMEM((TOP_K, tm, D // 2), U32), pltpu.SemaphoreType.DMA(())],
        compiler_params=_cp("arbitrary"), name="moe_combine")(
            slot.reshape(nblk, 1, tm * TOP_K), info, h, g.reshape(1, D), b.reshape(1, D), os_)


def _moe(hf, hp, p, alpha):
    R = hf.shape[0]
    G = MOE_BLOCK
    info, counts = _router(hf, p["wr_hi"], p["wr_lo"], p["br"])
    expert = info[:, :TOP_K].astype(I32)
    rank = info[:, 2 * TOP_K:3 * TOP_K].astype(I32)
    counts = counts[0, :N_EXPERTS].astype(I32)
    padded = ((counts + G - 1) // G) * G
    ends = jnp.cumsum(padded)
    slot = (ends - padded)[expert] + rank
    n_blocks = (R * TOP_K) // G + N_EXPERTS
    nb = (ends[-1] // G).astype(I32).reshape(1)
    block_expert = jnp.minimum(
        jnp.searchsorted(ends, jnp.arange(n_blocks, dtype=I32) * G, side="right"), N_EXPERTS - 1).astype(I32)
    tok = jnp.broadcast_to(jnp.arange(R, dtype=I32)[:, None], (R, TOP_K))
    tok_of_slot = jnp.zeros((n_blocks * G,), I32).at[slot.reshape(-1)].set(tok.reshape(-1))
    xs = _dispatch(hp, tok_of_slot, nb, n_blocks=n_blocks)
    os_ = _experts(block_expert, nb, xs, p["wg"], p["bg"], p["wu"], p["bu"], p["wd"], p["bd"], n_blocks=n_blocks)
    return _combine(slot, info, hf, p["ln2_g"], p["ln2_b"], os_, alpha)


def _prep_layer(l, lam_init, w):
    D = D_MODEL
    daw, hw = DA_HEADS * DA_V_DIM, HY_WIDTH
    sizes = (2 * DA_HEADS * DA_QK_DIM, 2 * DA_HEADS * DA_QK_DIM, daw, MLA_Q_RANK, MLA_KV_RANK, MLA_ROPE, 3 * hw)
    offs = [0]
    for s in sizes:
        offs.append(offs[-1] + s)
    w_in = w["w_in"][l]
    part = lambda i: w_in[:, offs[i]:offs[i + 1]]
    qa = part(0) * (DA_QK_DIM ** -0.5)
    p = {}
    p["w_in1"] = jnp.concatenate([part(3), part(6), qa, part(1), part(2)], axis=1).astype(BF16)
    p["w_in2"] = jnp.concatenate([part(4), part(5), jnp.zeros((D, LANE - MLA_ROPE), F32)], axis=1).astype(BF16)
    H = MLA_HEADS
    wq = w["w_uq"][l].reshape(MLA_Q_RANK, H, MLA_NOPE + MLA_ROPE)
    wq_rope = jnp.pad(wq[:, :, MLA_NOPE:], ((0, 0), (0, 0), (0, LANE - MLA_ROPE)))
    p["w_uq"] = jnp.concatenate([wq[:, :, :MLA_NOPE].reshape(MLA_Q_RANK, H * MLA_NOPE),
                                 wq_rope.reshape(MLA_Q_RANK, H * LANE)], axis=1).astype(BF16)
    wkv = w["w_ukv"][l].reshape(MLA_KV_RANK, H, MLA_NOPE + MLA_V)
    p["w_ukv"] = jnp.concatenate([wkv[:, :, :MLA_NOPE].reshape(MLA_KV_RANK, H * MLA_NOPE),
                                  wkv[:, :, MLA_NOPE:].reshape(MLA_KV_RANK, H * MLA_V)], axis=1).astype(BF16)
    p["q_norm_g"], p["kv_norm_g"] = w["mla_q_norm_g"][l], w["mla_kv_norm_g"][l]
    w_o = w["w_o"][l].astype(BF16)
    p["w_o"] = (w_o[:daw], w_o[daw:daw + H * MLA_V], w_o[daw + H * MLA_V:])
    p["mla_out_g"] = w["mla_out_g"][l]
    lv = w["da_lambda"][l]
    lam = jnp.exp(jnp.sum(lv[0] * lv[1])) - jnp.exp(jnp.sum(lv[2] * lv[3])) + lam_init
    slopes = jnp.exp2(-8.0 * jnp.arange(1, DA_HEADS + 1, dtype=F32) / DA_HEADS)
    p["da_scal"] = jnp.concatenate([slopes, lam.reshape(1), jnp.zeros((3,), F32)]).astype(F32)
    p["da_gain"] = w["da_norm_g"][l] * (1.0 - lam_init)
    for name in ("hy_short_w", "hy_short_b", "hy_d", "hy_out_g", "hy_ffn_w1", "hy_ffn_b1", "hy_ffn_w2",
                 "hy_ffn_b2", "hy_ffn_w3", "hy_ffn_b3", "hy_ffn_w4", "hy_freq", "ln1_g", "ln1_b", "ln2_g", "ln2_b"):
        p[name] = w[name][l]
    wr = jnp.pad(w["w_router"][l], ((0, 0), (0, LANE - N_EXPERTS)))
    p["wr_hi"] = wr.astype(BF16)
    p["wr_lo"] = (wr - p["wr_hi"].astype(F32)).astype(BF16)
    p["br"] = jnp.pad(w["b_router"][l], (0, LANE - N_EXPERTS), constant_values=NEG_BIG).reshape(1, LANE)
    p["wg"], p["wu"], p["wd"] = w["w_gate"][l].astype(BF16), w["w_up"][l].astype(BF16), w["w_down"][l].astype(BF16)
    p["bg"], p["bu"], p["bd"] = w["b_gate"][l], w["b_up"][l], w["b_down"][l]
    return p


def _layer(hf, hb, p, geo, consts, alpha):
    B, Lp, L = geo
    R = B * Lp
    fcat, rope_tabs = consts
    hw, daw = HY_WIDTH, DA_HEADS * DA_V_DIM
    tm = _tile(R, 1100)
    n1 = p["w_in1"].shape[1]
    wide1 = _mm(hb, p["w_in1"], tm=tm, tn=_tile(n1, 512, LANE), out_dtype=BF16, name="w_in_main")
    wide2 = _mm(hb, p["w_in2"], tm=tm, tn=p["w_in2"].shape[1], out_dtype=BF16, name="w_in_kv")
    hy_col, da_col = MLA_Q_RANK, MLA_Q_RANK + 3 * hw

    oa = _da_attention(wide1, p["da_scal"], p["da_gain"], B=B, Lp=Lp, L=L, col0=da_col)

    tm2 = _tile(R, 512)
    q = _rms_mm(wide1, p["q_norm_g"], p["w_uq"], tm=tm2, tn=_tile(p["w_uq"].shape[1], 1536, LANE), name="mla_q_up")
    kv = _rms_mm(wide2, p["kv_norm_g"], p["w_ukv"], tm=tm2, tn=_tile(p["w_ukv"].shape[1], 1536, LANE), name="mla_kv_up")
    ob = _mla_attention(q, kv, wide2, rope_tabs, B=B, Lp=Lp, L=L, kr_col=MLA_KV_RANK)

    z, x0c = _hy_pre(wide1, p["hy_short_w"], p["hy_short_b"], B=B, Lp=Lp, L=L, col0=hy_col)
    hf_taps, hb_taps = _hyena_taps(L, Lp, p["hy_ffn_w1"], p["hy_ffn_b1"], p["hy_ffn_w2"], p["hy_ffn_b2"],
                                   p["hy_ffn_w3"], p["hy_ffn_b3"], p["hy_ffn_w4"], p["hy_freq"])
    kr, ki = _hyena_spectrum(fcat, hf_taps, hb_taps, Lp)
    yr, yi = _hy_fwd(fcat, z, kr, ki, B=B, Lp=Lp)
    y = _hy_inv(fcat, yr, yi, B=B, Lp=Lp)
    oc = _hy_post(y, z, x0c, p["hy_d"], p["hy_out_g"])

    mix = _wo(oa, ob, oc, p["mla_out_g"], *p["w_o"])
    h1f, h1p = _ln_mix(hf, mix, p["ln1_g"], p["ln1_b"], alpha)
    return _moe(h1f, h1p, p, alpha)


def kernel(x_prompt, x_sample, meta_tokens, emb_ln_g, emb_ln_b, w_in, w_o, da_lambda, da_norm_g, mla_q_norm_g, mla_kv_norm_g, w_uq, w_ukv, mla_out_g, hy_short_w, hy_short_b, hy_ffn_w1, hy_ffn_b1, hy_ffn_w2, hy_ffn_b2, hy_ffn_w3, hy_ffn_b3, hy_ffn_w4, hy_freq, hy_d, hy_out_g, ln1_g, ln1_b, ln2_g, ln2_b, w_router, b_router, w_gate, b_gate, w_up, b_up, w_down, b_down):
    w = dict(w_in=w_in, w_o=w_o, da_lambda=da_lambda, da_norm_g=da_norm_g, mla_q_norm_g=mla_q_norm_g,
             mla_kv_norm_g=mla_kv_norm_g, w_uq=w_uq, w_ukv=w_ukv, mla_out_g=mla_out_g, hy_short_w=hy_short_w,
             hy_short_b=hy_short_b, hy_ffn_w1=hy_ffn_w1, hy_ffn_b1=hy_ffn_b1, hy_ffn_w2=hy_ffn_w2,
             hy_ffn_b2=hy_ffn_b2, hy_ffn_w3=hy_ffn_w3, hy_ffn_b3=hy_ffn_b3, hy_ffn_w4=hy_ffn_w4, hy_freq=hy_freq,
             hy_d=hy_d, hy_out_g=hy_out_g, ln1_g=ln1_g, ln1_b=ln1_b, ln2_g=ln2_g, ln2_b=ln2_b, w_router=w_router,
             b_router=b_router, w_gate=w_gate, b_gate=b_gate, w_up=w_up, b_up=b_up, w_down=w_down, b_down=b_down)
    depth = w_in.shape[0]
    alpha = (2 * depth) ** 0.25
    D = x_prompt.shape[-1]
    layers = [_prep_layer(l, 0.8 - 0.6 * math.exp(-0.3 * l), w) for l in range(depth)]

    outs = []
    for x in (x_prompt, x_sample):
        B, L0, _ = x.shape
        L = L0 + N_META
        Lp = -(-L // LANE) * LANE
        meta = jnp.broadcast_to(meta_tokens[None].astype(x.dtype), (B, N_META, D))
        rows = jnp.concatenate([meta, x, jnp.zeros((B, Lp - L, D), x.dtype)], axis=1).reshape(B * Lp, D)
        hf, hb = _ln_embed(rows, emb_ln_g, emb_ln_b)
        consts = (_dft_table(Lp), _rope_tables(Lp))
        for p in layers:
            hf, hb = _layer(hf, hb, p, (B, Lp, L), consts, alpha)
        outs.append(hf.reshape(B, Lp, D)[:, N_META:L])
    return tuple(outs)
```

```python
import functools
import math

import jax
import jax.numpy as jnp
from jax import lax
from jax.experimental import pallas as pl
from jax.experimental.pallas import tpu as pltpu

F32, BF16, I32, U32 = jnp.float32, jnp.bfloat16, jnp.int32, jnp.uint32

D_MODEL = 4096
DEPTH = 2
N_META = 16
LN_EPS = 1e-5
RMS_EPS = 1e-6
DA_HEADS = 12
DA_QK_DIM = 64
DA_V_DIM = 128
MLA_HEADS = 12
MLA_Q_RANK = 1024
MLA_KV_RANK = 512
MLA_NOPE = 128
MLA_ROPE = 64
MLA_V = 128
ROPE_THETA = 10000.0
HY_WIDTH = 1024
HY_BANDS = 16
HY_DECAY_TARGET = 1e-2
HY_DECAY_SHORT = 0.3
HY_DECAY_LONG = 1.5
N_EXPERTS = 32
TOP_K = 4
D_FF = 512
SWIGLU_LIMIT = 7.0
SWIGLU_ALPHA = 1.702

LANE = 128
VMEM_LIMIT = 56 * 1024 * 1024
MOE_BLOCK = 512
NEG_BIG = -1e30
FAR_POS = 1e9


def _cp(*sem):
    return pltpu.CompilerParams(dimension_semantics=sem, vmem_limit_bytes=VMEM_LIMIT)


def _tile(n, target, mult=16):
    best = None
    for d in range(mult, min(n, target) + 1, mult):
        if n % d == 0:
            best = d
    assert best is not None, (n, target, mult)
    return best


def _ln_math(x, g, b):
    mu = jnp.mean(x, axis=-1, keepdims=True)
    xc = x - mu
    var = jnp.mean(xc * xc, axis=-1, keepdims=True)
    return xc * lax.rsqrt(var + LN_EPS) * g + b


def _pack_pair(lo, hi):
    lo_b = lax.bitcast_convert_type(lo.astype(BF16).astype(F32), U32)
    hi_b = lax.bitcast_convert_type(hi.astype(BF16).astype(F32), U32)
    return (lo_b >> 16) | (hi_b & jnp.uint32(0xFFFF0000))


def _unpack_lo(w):
    return lax.bitcast_convert_type(w << 16, F32)


def _unpack_hi(w):
    return lax.bitcast_convert_type(w & jnp.uint32(0xFFFF0000), F32)


def _ln_embed_kernel(x_ref, g_ref, b_ref, of_ref, ob_ref):
    y = _ln_math(x_ref[...], g_ref[...], b_ref[...])
    of_ref[...] = y
    ob_ref[...] = y.astype(BF16)


def _ln_embed(x, g, b):
    R, D = x.shape
    tm = _tile(R, 256)
    row = pl.BlockSpec((tm, D), lambda i: (i, 0))
    vec = pl.BlockSpec((1, D), lambda i: (0, 0))
    return pl.pallas_call(
        _ln_embed_kernel, grid=(R // tm,), in_specs=[row, vec, vec], out_specs=[row, row],
        out_shape=[jax.ShapeDtypeStruct((R, D), F32), jax.ShapeDtypeStruct((R, D), BF16)],
        compiler_params=_cp("parallel"), name="ln_embed")(x, g.reshape(1, D), b.reshape(1, D))


def _ln_mix_kernel(h_ref, m_ref, g_ref, b_ref, of_ref, op_ref, *, alpha):
    y = _ln_math(alpha * h_ref[...] + m_ref[...], g_ref[...], b_ref[...])
    of_ref[...] = y
    half = y.shape[1] // 2
    op_ref[...] = _pack_pair(y[:, :half], y[:, half:])


def _ln_mix(h, mix, g, b, alpha):
    R, D = h.shape
    tm = _tile(R, 256)
    row = pl.BlockSpec((tm, D), lambda i: (i, 0))
    vec = pl.BlockSpec((1, D), lambda i: (0, 0))
    return pl.pallas_call(
        functools.partial(_ln_mix_kernel, alpha=alpha), grid=(R // tm,),
        in_specs=[row, row, vec, vec],
        out_specs=[row, pl.BlockSpec((tm, D // 2), lambda i: (i, 0))],
        out_shape=[jax.ShapeDtypeStruct((R, D), F32), jax.ShapeDtypeStruct((R, D // 2), U32)],
        compiler_params=_cp("parallel"), name="ln_mix")(h, mix, g.reshape(1, D), b.reshape(1, D))


def _mm_kernel(x_ref, w_ref, o_ref):
    o_ref[...] = jnp.dot(x_ref[...], w_ref[...], preferred_element_type=F32).astype(o_ref.dtype)


def _mm(x, w, *, tm, tn, out_dtype, k=None, x_kblock=0, name="mm"):
    M = x.shape[0]
    K, N = w.shape
    assert (k or x.shape[1]) == K and M % tm == 0 and N % tn == 0
    return pl.pallas_call(
        _mm_kernel, grid=(M // tm, N // tn),
        in_specs=[pl.BlockSpec((tm, K), lambda i, j: (i, x_kblock)),
                  pl.BlockSpec((K, tn), lambda i, j: (0, j))],
        out_specs=pl.BlockSpec((tm, tn), lambda i, j: (i, j)),
        out_shape=jax.ShapeDtypeStruct((M, N), out_dtype),
        compiler_params=_cp("parallel", "arbitrary"), name=name)(x, w)


def _rms_mm_kernel(x_ref, g_ref, w_ref, o_ref):
    x = x_ref[...].astype(F32)
    inv = lax.rsqrt(jnp.mean(x * x, axis=-1, keepdims=True) + RMS_EPS)
    xn = (x * inv * g_ref[...]).astype(BF16)
    o_ref[...] = jnp.dot(xn, w_ref[...], preferred_element_type=F32).astype(o_ref.dtype)


def _rms_mm(x, g, w, *, tm, tn, name):
    M = x.shape[0]
    K, N = w.shape
    assert M % tm == 0 and N % tn == 0
    return pl.pallas_call(
        _rms_mm_kernel, grid=(M // tm, N // tn),
        in_specs=[pl.BlockSpec((tm, K), lambda i, j: (i, 0)),
                  pl.BlockSpec((1, K), lambda i, j: (0, 0)),
                  pl.BlockSpec((K, tn), lambda i, j: (0, j))],
        out_specs=pl.BlockSpec((tm, tn), lambda i, j: (i, j)),
        out_shape=jax.ShapeDtypeStruct((M, N), BF16),
        compiler_params=_cp("parallel", "arbitrary"), name=name)(x, g.reshape(1, K), w)


def _wo_kernel(oa_ref, ob_ref, oc_ref, gb_ref, w1_ref, w2_ref, w3_ref, o_ref):
    ob = ob_ref[...].astype(F32)
    inv = lax.rsqrt(jnp.mean(ob * ob, axis=-1, keepdims=True) + RMS_EPS)
    obn = (ob * inv * gb_ref[...]).astype(BF16)
    acc = jnp.dot(oa_ref[...], w1_ref[...], preferred_element_type=F32)
    acc += jnp.dot(obn, w2_ref[...], preferred_element_type=F32)
    acc += jnp.dot(oc_ref[...], w3_ref[...], preferred_element_type=F32)
    o_ref[...] = acc


def _wo(oa, ob, oc, gb, w1, w2, w3):
    R = oa.shape[0]
    N = w1.shape[1]
    tm, tn = _tile(R, 512), _tile(N, 1024, LANE)
    xs = lambda a: pl.BlockSpec((tm, a.shape[1]), lambda j, i: (i, 0))
    ws = lambda a: pl.BlockSpec((a.shape[0], tn), lambda j, i: (0, j))
    return pl.pallas_call(
        _wo_kernel, grid=(N // tn, R // tm),
        in_specs=[xs(oa), xs(ob), xs(oc), pl.BlockSpec((1, ob.shape[1]), lambda j, i: (0, 0)),
                  ws(w1), ws(w2), ws(w3)],
        out_specs=pl.BlockSpec((tm, tn), lambda j, i: (i, j)),
        out_shape=jax.ShapeDtypeStruct((R, N), F32),
        compiler_params=_cp("parallel", "arbitrary"), name="w_o")(
            oa, ob, oc, gb.reshape(1, -1), w1, w2, w3)


ATT_SUB = 512


def _for_row_blocks(Lp, step):
    n_full = Lp // ATT_SUB
    if n_full:
        def body(r, c):
            step(pl.multiple_of(r * ATT_SUB, ATT_SUB), ATT_SUB)
            return c

        lax.fori_loop(0, n_full, body, 0)
    if Lp % ATT_SUB:
        step(n_full * ATT_SUB, Lp % ATT_SUB)


def _ones_lane_block(rows):
    lane = lax.broadcasted_iota(I32, (rows, LANE), 1)
    return jnp.where(lane == 0, 1.0, 0.0).astype(BF16)


def _softmax_pv(s, vaug):
    p = jnp.exp2(s - jnp.max(s, axis=-1, keepdims=True)).astype(BF16)
    oa = jnp.dot(p, vaug, preferred_element_type=F32)
    return oa[:, :LANE] / oa[:, LANE:LANE + 1]


def _da_kernel(sc_ref, q_ref, k_ref, v_ref, g_ref, o_ref, vaug_ref, *, L, n_heads):
    h = pl.program_id(1)
    Lp = k_ref.shape[0]
    slope, lam = sc_ref[h], sc_ref[n_heads]
    vaug_ref[:, :LANE] = v_ref[...]
    vaug_ref[:, LANE:] = _ones_lane_block(Lp)
    col = lax.broadcasted_iota(I32, (1, Lp), 1)
    kpos = jnp.where(col < L, col.astype(F32), FAR_POS) * slope
    dn = (((1,), (1,)), ((), ()))

    def step(row0, size):
        q = q_ref[pl.ds(row0, size), :]
        lane = lax.broadcasted_iota(I32, (size, LANE), 1)
        zero = jnp.zeros_like(q)
        q0 = jnp.where(lane < DA_QK_DIM, q, zero)
        q1 = jnp.where(lane >= DA_QK_DIM, q, zero)
        k = k_ref[...]
        qpos = (row0 + lax.broadcasted_iota(I32, (size, 1), 0)).astype(F32) * slope
        bias = jnp.abs(qpos - kpos)
        vaug = vaug_ref[...]
        o0 = _softmax_pv(lax.dot_general(q0, k, dn, preferred_element_type=F32) - bias, vaug)
        o1 = _softmax_pv(lax.dot_general(q1, k, dn, preferred_element_type=F32) - bias, vaug)
        o = o0 - lam * o1
        o = o * lax.rsqrt(jnp.mean(o * o, axis=-1, keepdims=True) + RMS_EPS) * g_ref[...]
        o_ref[pl.ds(row0, size), :] = o.astype(BF16)

    _for_row_blocks(Lp, step)


def _da_attention(wide, scal, gain, *, B, Lp, L, col0):
    H = DA_HEADS
    c0 = col0 // LANE
    blk = lambda part: pl.BlockSpec((Lp, LANE), lambda b, h: (b, c0 + part * H + h))
    return pl.pallas_call(
        functools.partial(_da_kernel, L=L, n_heads=H), grid=(B, H),
        in_specs=[pl.BlockSpec(memory_space=pltpu.SMEM), blk(0), blk(1), blk(2),
                  pl.BlockSpec((1, LANE), lambda b, h: (0, 0))],
        out_specs=pl.BlockSpec((Lp, LANE), lambda b, h: (b, h)),
        out_shape=jax.ShapeDtypeStruct((B * Lp, H * LANE), BF16),
        scratch_shapes=[pltpu.VMEM((Lp, 2 * LANE), BF16)],
        compiler_params=_cp("parallel", "parallel"), name="diff_attn")(
            scal, wide, wide, wide, gain.reshape(1, LANE))


def _rope_rot(x, c, s_up, s_dn):
    half = MLA_ROPE // 2
    return x * c + pltpu.roll(x, half, 1) * s_up + pltpu.roll(x, LANE - half, 1) * s_dn


def _mla_kernel(qn_ref, qr_ref, kn_ref, kr_ref, v_ref, c_ref, up_ref, dn_ref, o_ref,
                qcat_ref, kcat_ref, vaug_ref, *, L, scale):
    Lp = kn_ref.shape[0]
    c, s_up, s_dn = c_ref[...], up_ref[...], dn_ref[...]
    lane = lax.broadcasted_iota(I32, (Lp, LANE), 1)
    row = lax.broadcasted_iota(I32, (Lp, 1), 0)
    kr = _rope_rot(kr_ref[...].astype(F32), c, s_up, s_dn)
    kr = jnp.where(lane == MLA_ROPE, jnp.where(row < L, 0.0, NEG_BIG), kr)
    kcat_ref[:, :LANE] = kn_ref[...]
    kcat_ref[:, LANE:] = kr.astype(BF16)
    qr = _rope_rot(qr_ref[...].astype(F32), c, s_up, s_dn) * scale
    qr = jnp.where(lane == MLA_ROPE, 1.0, qr)
    qcat_ref[:, :LANE] = (qn_ref[...].astype(F32) * scale).astype(BF16)
    qcat_ref[:, LANE:] = qr.astype(BF16)
    vaug_ref[:, :LANE] = v_ref[...]
    vaug_ref[:, LANE:] = _ones_lane_block(Lp)

    def step(row0, size):
        s = lax.dot_general(qcat_ref[pl.ds(row0, size), :], kcat_ref[...], (((1,), (1,)), ((), ())),
                            preferred_element_type=F32)
        o_ref[pl.ds(row0, size), :] = _softmax_pv(s, vaug_ref[...]).astype(BF16)

    _for_row_blocks(Lp, step)


def _mla_attention(q, kv, wide2, tabs, *, B, Lp, L, kr_col):
    H = MLA_HEADS
    krb = kr_col // LANE
    blk = lambda f: pl.BlockSpec((Lp, LANE), f)
    tab = blk(lambda b, h: (0, 0))
    cat = pltpu.VMEM((Lp, 2 * LANE), BF16)
    scale = (MLA_NOPE + MLA_ROPE) ** -0.5 * math.log2(math.e)
    return pl.pallas_call(
        functools.partial(_mla_kernel, L=L, scale=scale), grid=(B, H),
        in_specs=[blk(lambda b, h: (b, h)), blk(lambda b, h: (b, H + h)), blk(lambda b, h: (b, h)),
                  blk(lambda b, h: (b, krb)), blk(lambda b, h: (b, H + h)), tab, tab, tab],
        out_specs=blk(lambda b, h: (b, h)),
        out_shape=jax.ShapeDtypeStruct((B * Lp, H * LANE), BF16),
        scratch_shapes=[cat, cat, cat],
        compiler_params=_cp("parallel", "parallel"), name="latent_attn")(
            q, q, kv, wide2, kv, *tabs)


def _rope_tables(Lp):
    half = MLA_ROPE // 2
    inv = ROPE_THETA ** (-jnp.arange(half, dtype=F32) / half)
    ang = jnp.arange(Lp, dtype=F32)[:, None] * inv[None, :]
    cos, sin, z = jnp.cos(ang), jnp.sin(ang), jnp.zeros((Lp, half), F32)
    return (jnp.concatenate([cos, cos, z, z], axis=1),
            jnp.concatenate([z, sin, z, z], axis=1),
            jnp.concatenate([-sin, z, z, z], axis=1))


def _hy_pre_kernel(x0_ref, x1_ref, v_ref, w0_ref, w1_ref, wv_ref, b0_ref, b1_ref, bv_ref,
                   z_ref, x0c_ref, *, L):
    Lp = x0_ref.shape[0]
    row = lax.broadcasted_iota(I32, (Lp, 1), 0)
    valid = row < L

    def short_conv(u_ref, w_ref, b_ref):
        u = jnp.where(valid, u_ref[...].astype(F32), 0.0)
        w = w_ref[...]
        return (pltpu.roll(u, 1, 0) * w[0:1] + u * w[1:2] + pltpu.roll(u, Lp - 1, 0) * w[2:3]
                + b_ref[...])

    x0c_ref[...] = short_conv(x0_ref, w0_ref, b0_ref).astype(BF16)
    z = short_conv(x1_ref, w1_ref, b1_ref) * short_conv(v_ref, wv_ref, bv_ref)
    z_ref[...] = jnp.where(valid, z, 0.0).astype(BF16)


def _hy_pre(wide, short_w, short_b, *, B, Lp, L, col0):
    assert Lp > L
    nc = HY_WIDTH // LANE
    c0 = col0 // LANE
    u = lambda part: pl.BlockSpec((Lp, LANE), lambda b, c: (b, c0 + part * nc + c))
    w = lambda part: pl.BlockSpec((3, LANE), lambda b, c: (0, part * nc + c))
    bb = lambda part: pl.BlockSpec((1, LANE), lambda b, c: (0, part * nc + c))
    out = pl.BlockSpec((Lp, LANE), lambda b, c: (b, c))
    shp = jax.ShapeDtypeStruct((B * Lp, HY_WIDTH), BF16)
    sb = short_b.reshape(1, -1)
    return pl.pallas_call(
        functools.partial(_hy_pre_kernel, L=L), grid=(B, nc),
        in_specs=[u(0), u(1), u(2), w(0), w(1), w(2), bb(0), bb(1), bb(2)],
        out_specs=[out, out], out_shape=[shp, shp],
        compiler_params=_cp("parallel", "parallel"), name="hyena_gate")(
            wide, wide, wide, short_w, short_w, short_w, sb, sb, sb)


def _hy_fwd_kernel(fc_ref, fs_ref, z_ref, kr_ref, ki_ref, yr_ref, yi_ref):
    z = z_ref[...]
    zr = jnp.dot(fc_ref[...], z, preferred_element_type=F32)
    zi = jnp.dot(fs_ref[...], z, preferred_element_type=F32)
    kr, ki = kr_ref[...], ki_ref[...]
    yr_ref[...] = (zr * kr - zi * ki).astype(BF16)
    yi_ref[...] = (zr * ki + zi * kr).astype(BF16)


def _hy_fwd(fcat, z, kr, ki, *, B, Lp):
    C = z.shape[1]
    tf, tn = _tile(Lp, 576), _tile(C, 512, LANE)
    nf = Lp // tf
    fspec = lambda half: pl.BlockSpec((tf, Lp), lambda i, b, j: (i, half))
    kspec = pl.BlockSpec((tf, tn), lambda i, b, j: (i, j))
    out = pl.BlockSpec((tf, tn), lambda i, b, j: (b * nf + i, j))
    shp = jax.ShapeDtypeStruct((B * Lp, C), BF16)
    return pl.pallas_call(
        _hy_fwd_kernel, grid=(nf, B, C // tn),
        in_specs=[fspec(0), fspec(1), pl.BlockSpec((Lp, tn), lambda i, b, j: (b, j)), kspec, kspec],
        out_specs=[out, out], out_shape=[shp, shp],
        compiler_params=_cp("parallel", "parallel", "arbitrary"), name="hyena_dft")(
            fcat, fcat, z, kr, ki)


def _hy_inv_kernel(fc_ref, fs_ref, yr_ref, yi_ref, o_ref):
    acc = jnp.dot(fc_ref[...], yr_ref[...], preferred_element_type=F32)
    acc += jnp.dot(fs_ref[...], yi_ref[...], preferred_element_type=F32)
    o_ref[...] = acc


def _hy_inv(fcat, yr, yi, *, B, Lp):
    C = yr.shape[1]
    tf, tn = _tile(Lp, 576), _tile(C, 512, LANE)
    nf = Lp // tf
    fspec = lambda half: pl.BlockSpec((tf, Lp), lambda i, b, j: (i, half))
    yspec = pl.BlockSpec((Lp, tn), lambda i, b, j: (b, j))
    return pl.pallas_call(
        _hy_inv_kernel, grid=(nf, B, C // tn),
        in_specs=[fspec(0), fspec(1), yspec, yspec],
        out_specs=pl.BlockSpec((tf, tn), lambda i, b, j: (b * nf + i, j)),
        out_shape=jax.ShapeDtypeStruct((B * Lp, C), F32),
        compiler_params=_cp("parallel", "parallel", "arbitrary"), name="hyena_idft")(
            fcat, fcat, yr, yi)


def _hy_post_kernel(y_ref, z_ref, x0_ref, d_ref, g_ref, o_ref):
    y = y_ref[...] + z_ref[...].astype(F32) * d_ref[...]
    o = x0_ref[...].astype(F32) * y
    o = o * lax.rsqrt(jnp.mean(o * o, axis=-1, keepdims=True) + RMS_EPS) * g_ref[...]
    o_ref[...] = o.astype(BF16)


def _hy_post(y, z, x0c, d, g):
    R, C = y.shape
    tm = _tile(R, 512)
    row = pl.BlockSpec((tm, C), lambda i: (i, 0))
    vec = pl.BlockSpec((1, C), lambda i: (0, 0))
    return pl.pallas_call(
        _hy_post_kernel, grid=(R // tm,), in_specs=[row, row, row, vec, vec], out_specs=row,
        out_shape=jax.ShapeDtypeStruct((R, C), BF16),
        compiler_params=_cp("parallel"), name="hyena_out")(y, z, x0c, d.reshape(1, C), g.reshape(1, C))


def _dft_table(Lp):
    N = 2 * Lp - 1
    blk = 64
    f = jnp.arange(Lp, dtype=I32)[:, None]
    th = 2.0 * math.pi / N
    a0 = ((f * jnp.arange(0, Lp, blk, dtype=I32)[None, :]) % N).astype(F32) * th
    a1 = ((f * jnp.arange(blk, dtype=I32)[None, :]) % N).astype(F32) * th
    c0, s0, c1, s1 = jnp.cos(a0)[:, :, None], jnp.sin(a0)[:, :, None], jnp.cos(a1)[:, None, :], jnp.sin(a1)[:, None, :]
    fc = (c0 * c1 - s0 * s1).reshape(Lp, Lp)
    fs = (s0 * c1 + c0 * s1).reshape(Lp, Lp)
    return jnp.concatenate([fc, -fs], axis=1).astype(BF16)


def _hyena_taps(L, Lp, w1, b1, w2, b2, w3, b3, w4, freq):
    hp = lax.Precision.HIGHEST
    t = jnp.linspace(0.0, 1.0, L, dtype=F32)[:, None]
    w = (2.0 * math.pi / L) * jnp.arange(L, dtype=F32)
    bands = jnp.linspace(1e-4, HY_BANDS - 1, HY_BANDS, dtype=F32)
    ang = w[:, None] * bands[None, :]
    z = jnp.concatenate([t, jnp.cos(ang), -jnp.sin(ang)], axis=-1)
    h = jnp.sin(freq * (jnp.dot(z, w1, precision=hp) + b1))
    h = jnp.sin(freq * (jnp.dot(h, w2, precision=hp) + b2))
    h = jnp.sin(freq * (jnp.dot(h, w3, precision=hp) + b3))
    h = jnp.dot(h, w4, precision=hp).reshape(L, 2, HY_WIDTH)
    deltas = jnp.linspace(math.log(HY_DECAY_TARGET) / HY_DECAY_LONG, math.log(HY_DECAY_TARGET) / HY_DECAY_SHORT,
                          HY_WIDTH, dtype=F32)
    h = h * jnp.exp(-t * jnp.abs(deltas))[:, None, :]
    hf = h[:, 0]
    hb = h[:, 1].at[0].set(0.0)
    norm = jnp.sum(jnp.abs(hf), axis=0, keepdims=True) + jnp.sum(jnp.abs(hb), axis=0, keepdims=True)
    pad = ((0, Lp - L), (0, 0))
    return jnp.pad(hf / norm, pad), jnp.pad(hb / norm, pad)


def _hyena_spectrum(fcat, hf, hb, Lp):
    N = 2 * Lp - 1
    tm, tn = _tile(Lp, 576), _tile(HY_WIDTH, 512, LANE)
    kr = _mm(fcat, (hf + hb).astype(BF16), tm=tm, tn=tn, out_dtype=F32, k=Lp, x_kblock=0, name="hyena_spec_re")
    ki = _mm(fcat, (hf - hb).astype(BF16), tm=tm, tn=tn, out_dtype=F32, k=Lp, x_kblock=1, name="hyena_spec_im")
    wf = jnp.where(jnp.arange(Lp)[:, None] == 0, 1.0 / N, 2.0 / N).astype(F32)
    return kr * wf, ki * wf


def _router_kernel(x_ref, wh_ref, wl_ref, b_ref, info_ref, cnt_ref, carry_ref):
    tm = x_ref.shape[0]

    @pl.when(pl.program_id(0) == 0)
    def _():
        carry_ref[...] = jnp.zeros_like(carry_ref)

    x = x_ref[...]
    xh = x.astype(BF16)
    xl = (x - xh.astype(F32)).astype(BF16)
    wh, wl = wh_ref[...], wl_ref[...]
    logits = (jnp.dot(xh, wh, preferred_element_type=F32) + jnp.dot(xl, wh, preferred_element_type=F32)
              + jnp.dot(xh, wl, preferred_element_type=F32)) + b_ref[...]
    lane = lax.broadcasted_iota(I32, (tm, LANE), 1)
    vals, hots = [], []
    cur = logits
    for _ in range(TOP_K):
        m = jnp.max(cur, axis=-1, keepdims=True)
        idx = jnp.min(jnp.where(cur == m, lane, LANE), axis=-1, keepdims=True)
        hot = lane == idx
        vals.append(m)
        hots.append(hot)
        cur = jnp.where(hot, NEG_BIG, cur)
    es = [jnp.exp(v - vals[0]) for v in vals]
    den = es[0] + es[1] + es[2] + es[3]
    multihot = (hots[0] | hots[1] | hots[2] | hots[3]).astype(F32)
    r_i = lax.broadcasted_iota(I32, (tm, tm), 0)
    c_i = lax.broadcasted_iota(I32, (tm, tm), 1)
    tri = (c_i < r_i).astype(BF16)
    before = jnp.dot(tri, multihot.astype(BF16), preferred_element_type=F32) + carry_ref[...]
    info = jnp.zeros((tm, LANE), F32)
    for kk in range(TOP_K):
        e_k = jnp.sum(jnp.where(hots[kk], lane, 0), axis=-1, keepdims=True).astype(F32)
        r_k = jnp.sum(jnp.where(hots[kk], before, 0.0), axis=-1, keepdims=True)
        info = jnp.where(lane == kk, e_k, info)
        info = jnp.where(lane == TOP_K + kk, es[kk] / den, info)
        info = jnp.where(lane == 2 * TOP_K + kk, r_k, info)
    info_ref[...] = info
    carry_ref[...] += jnp.sum(multihot, axis=0, keepdims=True)
    cnt_ref[...] = carry_ref[...]


def _router(x, wh, wl, b):
    R, D = x.shape
    tm = _tile(R, 512)
    return pl.pallas_call(
        _router_kernel, grid=(R // tm,),
        in_specs=[pl.BlockSpec((tm, D), lambda i: (i, 0)), pl.BlockSpec((D, LANE), lambda i: (0, 0)),
                  pl.BlockSpec((D, LANE), lambda i: (0, 0)), pl.BlockSpec((1, LANE), lambda i: (0, 0))],
        out_specs=[pl.BlockSpec((tm, LANE), lambda i: (i, 0)), pl.BlockSpec((1, LANE), lambda i: (0, 0))],
        out_shape=[jax.ShapeDtypeStruct((R, LANE), F32), jax.ShapeDtypeStruct((1, LANE), F32)],
        scratch_shapes=[pltpu.VMEM((1, LANE), F32)],
        compiler_params=_cp("arbitrary"), name="router")(x, wh, wl, b)


def _row_copy(src_hbm, dst_ref, src_row, dst_row, sem):
    return pltpu.make_async_copy(src_hbm.at[pl.ds(src_row, 1)], dst_ref.at[pl.ds(dst_row, 1)], sem)


def _experts_kernel(be_ref, nb_ref, tok_ref, tok_next_ref, x_hbm, wg_ref, bg_ref, wu_ref, bu_ref, wd_ref,
                    bd_ref, o_ref, xbuf_ref, sem):
    j = pl.program_id(0)
    nb = nb_ref[0]
    G, half = xbuf_ref.shape[1], xbuf_ref.shape[2]
    cur = lax.rem(j, 2)

    def gather(idx_ref, buf):
        def start(r, c):
            _row_copy(x_hbm, xbuf_ref.at[buf], idx_ref[0, 0, r], r, sem.at[buf]).start()
            return c

        lax.fori_loop(0, G, start, 0, unroll=8)

    @pl.when(j == 0)
    def _():
        gather(tok_ref, 0)

    @pl.when(j + 1 < nb)
    def _():
        gather(tok_next_ref, 1 - cur)

    @pl.when(j < nb)
    def _():
        pltpu.make_async_copy(x_hbm.at[pl.ds(0, G)], xbuf_ref.at[cur], sem.at[cur]).wait()
        xw = xbuf_ref[cur]
        xlo, xhi = _unpack_lo(xw).astype(BF16), _unpack_hi(xw).astype(BF16)

        def proj(w_ref, b_ref):
            return (jnp.dot(xlo, w_ref[0, :half], preferred_element_type=F32)
                    + jnp.dot(xhi, w_ref[0, half:], preferred_element_type=F32) + b_ref[0])

        g = jnp.minimum(proj(wg_ref, bg_ref), SWIGLU_LIMIT)
        u = jnp.clip(proj(wu_ref, bu_ref), -SWIGLU_LIMIT, SWIGLU_LIMIT)
        hid = ((u + 1.0) * g * jax.nn.sigmoid(SWIGLU_ALPHA * g)).astype(BF16)
        o = jnp.dot(hid, wd_ref[0], preferred_element_type=F32) + bd_ref[0]
        o_ref[...] = _pack_pair(o[:, :half], o[:, half:])

    @pl.when(j >= nb)
    def _():
        o_ref[...] = jnp.zeros_like(o_ref)


def _experts(block_expert, nb, tok_of_slot, xp, wg, bg, wu, bu, wd, bd, *, n_blocks):
    G = MOE_BLOCK
    E, D, F = wg.shape
    W = xp.shape[1]
    wsp = lambda shape: pl.BlockSpec((1,) + shape, lambda j, be, nb: (be[j], 0, 0))
    toks = tok_of_slot.reshape(n_blocks, 1, G)
    tsp = lambda d: pl.BlockSpec((1, 1, G), lambda j, be, nb: (jnp.minimum(j + d, n_blocks - 1), 0, 0),
                                 memory_space=pltpu.SMEM)
    return pl.pallas_call(
        _experts_kernel,
        grid_spec=pltpu.PrefetchScalarGridSpec(
            num_scalar_prefetch=2, grid=(n_blocks,),
            in_specs=[tsp(0), tsp(1), pl.BlockSpec(memory_space=pl.ANY),
                      wsp((D, F)), wsp((1, F)), wsp((D, F)), wsp((1, F)), wsp((F, D)), wsp((1, D))],
            out_specs=pl.BlockSpec((G, W), lambda j, be, nb: (j, 0)),
            scratch_shapes=[pltpu.VMEM((2, G, W), U32), pltpu.SemaphoreType.DMA((2,))]),
        out_shape=jax.ShapeDtypeStruct((n_blocks * G, W), U32),
        compiler_params=_cp("arbitrary"), name="moe_experts")(
            block_expert, nb, toks, toks, xp, wg, bg.reshape(E, 1, F), wu, bu.reshape(E, 1, F), wd,
            bd.reshape(E, 1, D))


def _combine_kernel(slot_ref, slot_next_ref, w_ref, h_ref, g_ref, b_ref, os_hbm, of_ref, ob_ref, buf_ref, sem,
                    *, alpha):
    i = pl.program_id(0)
    tm = h_ref.shape[0]
    cur = lax.rem(i, 2)

    def gather(idx_ref, buf):
        def start(r, c):
            for kk in range(TOP_K):
                _row_copy(os_hbm, buf_ref.at[buf, kk], idx_ref[0, 0, r * TOP_K + kk], r, sem.at[buf]).start()
            return c

        lax.fori_loop(0, tm, start, 0, unroll=4)

    @pl.when(i == 0)
    def _():
        gather(slot_ref, 0)

    @pl.when(i + 1 < pl.num_programs(0))
    def _():
        gather(slot_next_ref, 1 - cur)

    for kk in range(TOP_K):
        pltpu.make_async_copy(os_hbm.at[pl.ds(0, tm)], buf_ref.at[cur, kk], sem.at[cur]).wait()
    w = w_ref[...]
    lo = jnp.zeros((tm, buf_ref.shape[3]), F32)
    hi = jnp.zeros((tm, buf_ref.shape[3]), F32)
    for kk in range(TOP_K):
        word = buf_ref[cur, kk]
        wk = w[:, TOP_K + kk:TOP_K + kk + 1]
        lo += wk * _unpack_lo(word)
        hi += wk * _unpack_hi(word)
    y = _ln_math(alpha * h_ref[...] + jnp.concatenate([lo, hi], axis=1), g_ref[...], b_ref[...])
    of_ref[...] = y
    ob_ref[...] = y.astype(BF16)


def _combine(slot, info, h, g, b, os_, alpha):
    R, D = h.shape
    tm = _tile(R, 128)
    nblk = R // tm
    row = pl.BlockSpec((tm, D), lambda i: (i, 0))
    vec = pl.BlockSpec((1, D), lambda i: (0, 0))
    slots = slot.reshape(nblk, 1, tm * TOP_K)
    ssp = lambda d: pl.BlockSpec((1, 1, tm * TOP_K), lambda i: (jnp.minimum(i + d, nblk - 1), 0, 0),
                                 memory_space=pltpu.SMEM)
    return pl.pallas_call(
        functools.partial(_combine_kernel, alpha=alpha), grid=(nblk,),
        in_specs=[ssp(0), ssp(1), pl.BlockSpec((tm, LANE), lambda i: (i, 0)), row, vec, vec,
                  pl.BlockSpec(memory_space=pl.ANY)],
        out_specs=[row, row],
        out_shape=[jax.ShapeDtypeStruct((R, D), F32), jax.ShapeDtypeStruct((R, D), BF16)],
        scratch_shapes=[pltpu.VMEM((2, TOP_K, tm, D // 2), U32), pltpu.SemaphoreType.DMA((2,))],
        compiler_params=_cp("arbitrary"), name="moe_combine")(
            slots, slots, info, h, g.reshape(1, D), b.reshape(1, D), os_)


def _moe(hf, hp, p, alpha):
    R = hf.shape[0]
    G = MOE_BLOCK
    info, counts = _router(hf, p["wr_hi"], p["wr_lo"], p["br"])
    expert = info[:, :TOP_K].astype(I32)
    rank = info[:, 2 * TOP_K:3 * TOP_K].astype(I32)
    counts = counts[0, :N_EXPERTS].astype(I32)
    padded = ((counts + G - 1) // G) * G
    ends = jnp.cumsum(padded)
    slot = (ends - padded)[expert] + rank
    n_blocks = -(-(R * TOP_K) // G) + N_EXPERTS
    nb = (ends[-1] // G).astype(I32).reshape(1)
    starts = jnp.arange(n_blocks, dtype=I32)[:, None] * G
    block_expert = jnp.minimum(jnp.sum((ends[None, :] <= starts).astype(I32), axis=1), N_EXPERTS - 1)
    tok = jnp.broadcast_to(jnp.arange(R, dtype=I32)[:, None], (R, TOP_K))
    tok_of_slot = jnp.zeros((n_blocks * G,), I32).at[slot.reshape(-1)].set(tok.reshape(-1))
    os_ = _experts(block_expert, nb, tok_of_slot, hp, p["wg"], p["bg"], p["wu"], p["bu"], p["wd"], p["bd"],
                   n_blocks=n_blocks)
    return _combine(slot, info, hf, p["ln2_g"], p["ln2_b"], os_, alpha)


def _prep_layer(l, lam_init, w):
    D = D_MODEL
    daw, hw = DA_HEADS * DA_V_DIM, HY_WIDTH
    sizes = (2 * DA_HEADS * DA_QK_DIM, 2 * DA_HEADS * DA_QK_DIM, daw, MLA_Q_RANK, MLA_KV_RANK, MLA_ROPE, 3 * hw)
    offs = [0]
    for s in sizes:
        offs.append(offs[-1] + s)
    w_in = w["w_in"][l]
    part = lambda i: w_in[:, offs[i]:offs[i + 1]]
    log2e = math.log2(math.e)
    qa = part(0) * (DA_QK_DIM ** -0.5 * log2e)
    p = {}
    p["w_in1"] = jnp.concatenate([part(3), part(6), qa, part(1), part(2)], axis=1).astype(BF16)
    p["w_in2"] = jnp.concatenate([part(4), part(5), jnp.zeros((D, LANE - MLA_ROPE), F32)], axis=1).astype(BF16)
    H = MLA_HEADS
    wq = w["w_uq"][l].reshape(MLA_Q_RANK, H, MLA_NOPE + MLA_ROPE)
    wq_rope = jnp.pad(wq[:, :, MLA_NOPE:], ((0, 0), (0, 0), (0, LANE - MLA_ROPE)))
    p["w_uq"] = jnp.concatenate([wq[:, :, :MLA_NOPE].reshape(MLA_Q_RANK, H * MLA_NOPE),
                                 wq_rope.reshape(MLA_Q_RANK, H * LANE)], axis=1).astype(BF16)
    wkv = w["w_ukv"][l].reshape(MLA_KV_RANK, H, MLA_NOPE + MLA_V)
    p["w_ukv"] = jnp.concatenate([wkv[:, :, :MLA_NOPE].reshape(MLA_KV_RANK, H * MLA_NOPE),
                                  wkv[:, :, MLA_NOPE:].reshape(MLA_KV_RANK, H * MLA_V)], axis=1).astype(BF16)
    p["q_norm_g"], p["kv_norm_g"] = w["mla_q_norm_g"][l], w["mla_kv_norm_g"][l]
    w_o = w["w_o"][l].astype(BF16)
    p["w_o"] = (w_o[:daw], w_o[daw:daw + H * MLA_V], w_o[daw + H * MLA_V:])
    p["mla_out_g"] = w["mla_out_g"][l]
    lv = w["da_lambda"][l]
    lam = jnp.exp(jnp.sum(lv[0] * lv[1])) - jnp.exp(jnp.sum(lv[2] * lv[3])) + lam_init
    slopes = jnp.exp2(-8.0 * jnp.arange(1, DA_HEADS + 1, dtype=F32) / DA_HEADS)
    p["da_scal"] = jnp.concatenate([slopes * log2e, lam.reshape(1), jnp.zeros((3,), F32)]).astype(F32)
    p["da_gain"] = w["da_norm_g"][l] * (1.0 - lam_init)
    for name in ("hy_short_w", "hy_short_b", "hy_d", "hy_out_g", "hy_ffn_w1", "hy_ffn_b1", "hy_ffn_w2",
                 "hy_ffn_b2", "hy_ffn_w3", "hy_ffn_b3", "hy_ffn_w4", "hy_freq", "ln1_g", "ln1_b", "ln2_g", "ln2_b"):
        p[name] = w[name][l]
    wr = jnp.pad(w["w_router"][l], ((0, 0), (0, LANE - N_EXPERTS)))
    p["wr_hi"] = wr.astype(BF16)
    p["wr_lo"] = (wr - p["wr_hi"].astype(F32)).astype(BF16)
    p["br"] = jnp.pad(w["b_router"][l], (0, LANE - N_EXPERTS), constant_values=NEG_BIG).reshape(1, LANE)
    p["wg"], p["wu"], p["wd"] = w["w_gate"][l].astype(BF16), w["w_up"][l].astype(BF16), w["w_down"][l].astype(BF16)
    p["bg"], p["bu"], p["bd"] = w["b_gate"][l], w["b_up"][l], w["b_down"][l]
    return p


def _layer(hf, hb, p, geo, consts, alpha):
    B, Lp, L = geo
    R = B * Lp
    fcat, rope_tabs = consts
    hw, daw = HY_WIDTH, DA_HEADS * DA_V_DIM
    tm = _tile(R, 1100)
    n1 = p["w_in1"].shape[1]
    wide1 = _mm(hb, p["w_in1"], tm=tm, tn=_tile(n1, 512, LANE), out_dtype=BF16, name="w_in_main")
    wide2 = _mm(hb, p["w_in2"], tm=tm, tn=p["w_in2"].shape[1], out_dtype=BF16, name="w_in_kv")
    hy_col, da_col = MLA_Q_RANK, MLA_Q_RANK + 3 * hw

    oa = _da_attention(wide1, p["da_scal"], p["da_gain"], B=B, Lp=Lp, L=L, col0=da_col)

    tm2 = _tile(R, 512)
    q = _rms_mm(wide1, p["q_norm_g"], p["w_uq"], tm=tm2, tn=_tile(p["w_uq"].shape[1], 1536, LANE), name="mla_q_up")
    kv = _rms_mm(wide2, p["kv_norm_g"], p["w_ukv"], tm=tm2, tn=_tile(p["w_ukv"].shape[1], 1536, LANE), name="mla_kv_up")
    ob = _mla_attention(q, kv, wide2, rope_tabs, B=B, Lp=Lp, L=L, kr_col=MLA_KV_RANK)

    z, x0c = _hy_pre(wide1, p["hy_short_w"], p["hy_short_b"], B=B, Lp=Lp, L=L, col0=hy_col)
    hf_taps, hb_taps = _hyena_taps(L, Lp, p["hy_ffn_w1"], p["hy_ffn_b1"], p["hy_ffn_w2"], p["hy_ffn_b2"],
                                   p["hy_ffn_w3"], p["hy_ffn_b3"], p["hy_ffn_w4"], p["hy_freq"])
    kr, ki = _hyena_spectrum(fcat, hf_taps, hb_taps, Lp)
    yr, yi = _hy_fwd(fcat, z, kr, ki, B=B, Lp=Lp)
    y = _hy_inv(fcat, yr, yi, B=B, Lp=Lp)
    oc = _hy_post(y, z, x0c, p["hy_d"], p["hy_out_g"])

    mix = _wo(oa, ob, oc, p["mla_out_g"], *p["w_o"])
    h1f, h1p = _ln_mix(hf, mix, p["ln1_g"], p["ln1_b"], alpha)
    return _moe(h1f, h1p, p, alpha)


def kernel(x_prompt, x_sample, meta_tokens, emb_ln_g, emb_ln_b, w_in, w_o, da_lambda, da_norm_g, mla_q_norm_g, mla_kv_norm_g, w_uq, w_ukv, mla_out_g, hy_short_w, hy_short_b, hy_ffn_w1, hy_ffn_b1, hy_ffn_w2, hy_ffn_b2, hy_ffn_w3, hy_ffn_b3, hy_ffn_w4, hy_freq, hy_d, hy_out_g, ln1_g, ln1_b, ln2_g, ln2_b, w_router, b_router, w_gate, b_gate, w_up, b_up, w_down, b_down):
    w = dict(w_in=w_in, w_o=w_o, da_lambda=da_lambda, da_norm_g=da_norm_g, mla_q_norm_g=mla_q_norm_g,
             mla_kv_norm_g=mla_kv_norm_g, w_uq=w_uq, w_ukv=w_ukv, mla_out_g=mla_out_g, hy_short_w=hy_short_w,
             hy_short_b=hy_short_b, hy_ffn_w1=hy_ffn_w1, hy_ffn_b1=hy_ffn_b1, hy_ffn_w2=hy_ffn_w2,
             hy_ffn_b2=hy_ffn_b2, hy_ffn_w3=hy_ffn_w3, hy_ffn_b3=hy_ffn_b3, hy_ffn_w4=hy_ffn_w4, hy_freq=hy_freq,
             hy_d=hy_d, hy_out_g=hy_out_g, ln1_g=ln1_g, ln1_b=ln1_b, ln2_g=ln2_g, ln2_b=ln2_b, w_router=w_router,
             b_router=b_router, w_gate=w_gate, b_gate=b_gate, w_up=w_up, b_up=b_up, w_down=w_down, b_down=b_down)
    depth = w_in.shape[0]
    alpha = (2 * depth) ** 0.25
    D = x_prompt.shape[-1]
    layers = [_prep_layer(l, 0.8 - 0.6 * math.exp(-0.3 * l), w) for l in range(depth)]

    outs = []
    for x in (x_prompt, x_sample):
        B, L0, _ = x.shape
        L = L0 + N_META
        Lp = -(-L // LANE) * LANE
        meta = jnp.broadcast_to(meta_tokens[None].astype(x.dtype), (B, N_META, D))
        rows = jnp.concatenate([meta, x, jnp.zeros((B, Lp - L, D), x.dtype)], axis=1).reshape(B * Lp, D)
        hf, hb = _ln_embed(rows, emb_ln_g, emb_ln_b)
        consts = (_dft_table(Lp), _rope_tables(Lp))
        for p in layers:
            hf, hb = _layer(hf, hb, p, (B, Lp, L), consts, alpha)
        outs.append(hf.reshape(B, Lp, D)[:, N_META:L])
    return tuple(outs)
```

```python
import functools
import math

import jax
import jax.numpy as jnp
from jax import lax
from jax.experimental import pallas as pl
from jax.experimental.pallas import tpu as pltpu

F32, BF16, I32, U32 = jnp.float32, jnp.bfloat16, jnp.int32, jnp.uint32

D_MODEL = 4096
DEPTH = 2
N_META = 16
LN_EPS = 1e-5
RMS_EPS = 1e-6
DA_HEADS = 12
DA_QK_DIM = 64
DA_V_DIM = 128
MLA_HEADS = 12
MLA_Q_RANK = 1024
MLA_KV_RANK = 512
MLA_NOPE = 128
MLA_ROPE = 64
MLA_V = 128
ROPE_THETA = 10000.0
HY_WIDTH = 1024
HY_BANDS = 16
HY_DECAY_TARGET = 1e-2
HY_DECAY_SHORT = 0.3
HY_DECAY_LONG = 1.5
N_EXPERTS = 32
TOP_K = 4
D_FF = 512
SWIGLU_LIMIT = 7.0
SWIGLU_ALPHA = 1.702

LANE = 128
VMEM_LIMIT = 56 * 1024 * 1024
MOE_BLOCK = 512
NEG_BIG = -1e30
FAR_POS = 1e9


def _cp(*sem):
    return pltpu.CompilerParams(dimension_semantics=sem, vmem_limit_bytes=VMEM_LIMIT)


def _tile(n, target, mult=16):
    best = None
    for d in range(mult, min(n, target) + 1, mult):
        if n % d == 0:
            best = d
    assert best is not None, (n, target, mult)
    return best


def _ln_math(x, g, b):
    mu = jnp.mean(x, axis=-1, keepdims=True)
    xc = x - mu
    var = jnp.mean(xc * xc, axis=-1, keepdims=True)
    return xc * lax.rsqrt(var + LN_EPS) * g + b


def _pack_pair(lo, hi):
    lo_b = lax.bitcast_convert_type(lo.astype(BF16).astype(F32), U32)
    hi_b = lax.bitcast_convert_type(hi.astype(BF16).astype(F32), U32)
    return (lo_b >> 16) | (hi_b & jnp.uint32(0xFFFF0000))


def _unpack_lo(w):
    return lax.bitcast_convert_type(w << 16, F32)


def _unpack_hi(w):
    return lax.bitcast_convert_type(w & jnp.uint32(0xFFFF0000), F32)


def _ln_embed_kernel(x_ref, g_ref, b_ref, of_ref, ob_ref):
    y = _ln_math(x_ref[...], g_ref[...], b_ref[...])
    of_ref[...] = y
    ob_ref[...] = y.astype(BF16)


def _ln_embed(x, g, b):
    R, D = x.shape
    tm = _tile(R, 256)
    row = pl.BlockSpec((tm, D), lambda i: (i, 0))
    vec = pl.BlockSpec((1, D), lambda i: (0, 0))
    return pl.pallas_call(
        _ln_embed_kernel, grid=(R // tm,), in_specs=[row, vec, vec], out_specs=[row, row],
        out_shape=[jax.ShapeDtypeStruct((R, D), F32), jax.ShapeDtypeStruct((R, D), BF16)],
        compiler_params=_cp("parallel"), name="ln_embed")(x, g.reshape(1, D), b.reshape(1, D))


def _ln_mix_kernel(h_ref, m_ref, g_ref, b_ref, of_ref, op_ref, *, alpha):
    y = _ln_math(alpha * h_ref[...] + m_ref[...], g_ref[...], b_ref[...])
    of_ref[...] = y
    half = y.shape[1] // 2
    op_ref[...] = _pack_pair(y[:, :half], y[:, half:])


def _ln_mix(h, mix, g, b, alpha):
    R, D = h.shape
    tm = _tile(R, 256)
    row = pl.BlockSpec((tm, D), lambda i: (i, 0))
    vec = pl.BlockSpec((1, D), lambda i: (0, 0))
    return pl.pallas_call(
        functools.partial(_ln_mix_kernel, alpha=alpha), grid=(R // tm,),
        in_specs=[row, row, vec, vec],
        out_specs=[row, pl.BlockSpec((tm, D // 2), lambda i: (i, 0))],
        out_shape=[jax.ShapeDtypeStruct((R, D), F32), jax.ShapeDtypeStruct((R, D // 2), U32)],
        compiler_params=_cp("parallel"), name="ln_mix")(h, mix, g.reshape(1, D), b.reshape(1, D))


def _mm_kernel(x_ref, w_ref, o_ref):
    o_ref[...] = jnp.dot(x_ref[...], w_ref[...], preferred_element_type=F32).astype(o_ref.dtype)


def _mm(x, w, *, tm, tn, out_dtype, k=None, x_kblock=0, name="mm"):
    M = x.shape[0]
    K, N = w.shape
    assert (k or x.shape[1]) == K and M % tm == 0 and N % tn == 0
    return pl.pallas_call(
        _mm_kernel, grid=(M // tm, N // tn),
        in_specs=[pl.BlockSpec((tm, K), lambda i, j: (i, x_kblock)),
                  pl.BlockSpec((K, tn), lambda i, j: (0, j))],
        out_specs=pl.BlockSpec((tm, tn), lambda i, j: (i, j)),
        out_shape=jax.ShapeDtypeStruct((M, N), out_dtype),
        compiler_params=_cp("parallel", "arbitrary"), name=name)(x, w)


def _rms_mm_kernel(x_ref, g_ref, w_ref, o_ref):
    x = x_ref[...].astype(F32)
    inv = lax.rsqrt(jnp.mean(x * x, axis=-1, keepdims=True) + RMS_EPS)
    xn = (x * inv * g_ref[...]).astype(BF16)
    o_ref[...] = jnp.dot(xn, w_ref[...], preferred_element_type=F32).astype(o_ref.dtype)


def _rms_mm(x, g, w, *, tm, tn, name):
    M = x.shape[0]
    K, N = w.shape
    assert M % tm == 0 and N % tn == 0
    return pl.pallas_call(
        _rms_mm_kernel, grid=(M // tm, N // tn),
        in_specs=[pl.BlockSpec((tm, K), lambda i, j: (i, 0)),
                  pl.BlockSpec((1, K), lambda i, j: (0, 0)),
                  pl.BlockSpec((K, tn), lambda i, j: (0, j))],
        out_specs=pl.BlockSpec((tm, tn), lambda i, j: (i, j)),
        out_shape=jax.ShapeDtypeStruct((M, N), BF16),
        compiler_params=_cp("parallel", "arbitrary"), name=name)(x, g.reshape(1, K), w)


def _wo_kernel(oa_ref, ob_ref, oc_ref, gb_ref, w1_ref, w2_ref, w3_ref, o_ref):
    ob = ob_ref[...].astype(F32)
    inv = lax.rsqrt(jnp.mean(ob * ob, axis=-1, keepdims=True) + RMS_EPS)
    obn = (ob * inv * gb_ref[...]).astype(BF16)
    acc = jnp.dot(oa_ref[...], w1_ref[...], preferred_element_type=F32)
    acc += jnp.dot(obn, w2_ref[...], preferred_element_type=F32)
    acc += jnp.dot(oc_ref[...], w3_ref[...], preferred_element_type=F32)
    o_ref[...] = acc


def _wo(oa, ob, oc, gb, w1, w2, w3):
    R = oa.shape[0]
    N = w1.shape[1]
    tm, tn = _tile(R, 512), _tile(N, 1024, LANE)
    xs = lambda a: pl.BlockSpec((tm, a.shape[1]), lambda j, i: (i, 0))
    ws = lambda a: pl.BlockSpec((a.shape[0], tn), lambda j, i: (0, j))
    return pl.pallas_call(
        _wo_kernel, grid=(N // tn, R // tm),
        in_specs=[xs(oa), xs(ob), xs(oc), pl.BlockSpec((1, ob.shape[1]), lambda j, i: (0, 0)),
                  ws(w1), ws(w2), ws(w3)],
        out_specs=pl.BlockSpec((tm, tn), lambda j, i: (i, j)),
        out_shape=jax.ShapeDtypeStruct((R, N), F32),
        compiler_params=_cp("parallel", "arbitrary"), name="w_o")(
            oa, ob, oc, gb.reshape(1, -1), w1, w2, w3)


ATT_SUB = 256


def _for_row_blocks(Lp, step):
    n_full = Lp // ATT_SUB
    if n_full:
        def body(r, c):
            step(pl.multiple_of(r * ATT_SUB, ATT_SUB), ATT_SUB)
            return c

        lax.fori_loop(0, n_full, body, 0, unroll=2)
    if Lp % ATT_SUB:
        step(n_full * ATT_SUB, Lp % ATT_SUB)


def _pipeline_row_blocks(Lp, probs, consume):
    S = ATT_SUB
    n_full, tail = Lp // S, Lp % S
    if n_full < 2 or n_full % 2:
        def step(row0, size):
            probs(row0, size, 0)
            consume(row0, size, 0)

        _for_row_blocks(Lp, step)
        return
    probs(0, S, 0)

    def body(i, c):
        r0 = pl.multiple_of(i * (2 * S), 2 * S)
        r1 = pl.multiple_of(r0 + S, S)
        probs(r1, S, 1)
        consume(r0, S, 0)
        probs(pl.multiple_of(r0 + 2 * S, 2 * S), S, 0)
        consume(r1, S, 1)
        return c

    lax.fori_loop(0, n_full // 2 - 1, body, 0)
    r0 = (n_full - 2) * S
    probs(r0 + S, S, 1)
    consume(r0, S, 0)
    if tail:
        probs(n_full * S, tail, 2)
    consume(r0 + S, S, 1)
    if tail:
        consume(n_full * S, tail, 2)


def _ones_lane_block(rows):
    lane = lax.broadcasted_iota(I32, (rows, LANE), 1)
    return jnp.where(lane == 0, 1.0, 0.0).astype(BF16)


def _pv(p, vaug):
    oa = jnp.dot(p, vaug, preferred_element_type=F32)
    return oa[:, :LANE] / oa[:, LANE:LANE + 1]


def _da_kernel(sc_ref, q_ref, k_ref, v_ref, g_ref, o_ref, vaug_ref, p_ref, *, L, n_heads):
    h = pl.program_id(1)
    Lp = k_ref.shape[0]
    slope, lam = sc_ref[h], sc_ref[n_heads]
    vaug_ref[:, :LANE] = v_ref[...]
    vaug_ref[:, LANE:] = _ones_lane_block(Lp)
    col = lax.broadcasted_iota(I32, (1, Lp), 1)
    kpos = jnp.where(col < L, col.astype(F32), FAR_POS) * slope
    dn = (((1,), (1,)), ((), ()))

    def probs(row0, size, buf):
        q = q_ref[pl.ds(row0, size), :]
        lane = lax.broadcasted_iota(I32, (size, LANE), 1)
        zero = jnp.zeros_like(q)
        k = k_ref[...]
        qpos = (row0 + lax.broadcasted_iota(I32, (size, 1), 0)).astype(F32) * slope
        bias = jnp.abs(qpos - kpos)
        for m, qm in enumerate((jnp.where(lane < DA_QK_DIM, q, zero), jnp.where(lane >= DA_QK_DIM, q, zero))):
            s = lax.dot_general(qm, k, dn, preferred_element_type=F32) - bias
            p_ref[buf, m, :size, :] = jnp.exp2(s - jnp.max(s, axis=-1, keepdims=True)).astype(BF16)

    def consume(row0, size, buf):
        vaug = vaug_ref[...]
        o = _pv(p_ref[buf, 0, :size, :], vaug) - lam * _pv(p_ref[buf, 1, :size, :], vaug)
        o = o * lax.rsqrt(jnp.mean(o * o, axis=-1, keepdims=True) + RMS_EPS) * g_ref[...]
        o_ref[pl.ds(row0, size), :] = o.astype(BF16)

    _pipeline_row_blocks(Lp, probs, consume)


def _da_attention(wide, scal, gain, *, B, Lp, L, col0):
    H = DA_HEADS
    c0 = col0 // LANE
    blk = lambda part: pl.BlockSpec((Lp, LANE), lambda b, h: (b, c0 + part * H + h))
    return pl.pallas_call(
        functools.partial(_da_kernel, L=L, n_heads=H), grid=(B, H),
        in_specs=[pl.BlockSpec(memory_space=pltpu.SMEM), blk(0), blk(1), blk(2),
                  pl.BlockSpec((1, LANE), lambda b, h: (0, 0))],
        out_specs=pl.BlockSpec((Lp, LANE), lambda b, h: (b, h)),
        out_shape=jax.ShapeDtypeStruct((B * Lp, H * LANE), BF16),
        scratch_shapes=[pltpu.VMEM((Lp, 2 * LANE), BF16), pltpu.VMEM((3, 2, min(ATT_SUB, Lp), Lp), BF16)],
        compiler_params=_cp("parallel", "parallel"), name="diff_attn")(
            scal, wide, wide, wide, gain.reshape(1, LANE))


def _rope_rot(x, c, s_up, s_dn):
    half = MLA_ROPE // 2
    return x * c + pltpu.roll(x, half, 1) * s_up + pltpu.roll(x, LANE - half, 1) * s_dn


def _mla_kernel(qn_ref, qr_ref, kn_ref, kr_ref, v_ref, c_ref, up_ref, dn_ref, o_ref,
                qcat_ref, kcat_ref, vaug_ref, p_ref, *, L):
    Lp = kn_ref.shape[0]
    c, s_up, s_dn = c_ref[...], up_ref[...], dn_ref[...]
    lane = lax.broadcasted_iota(I32, (Lp, LANE), 1)

    @pl.when(pl.program_id(1) == 0)
    def _():
        row = lax.broadcasted_iota(I32, (Lp, 1), 0)
        kr = _rope_rot(kr_ref[...].astype(F32), c, s_up, s_dn)
        kr = jnp.where(lane == MLA_ROPE, jnp.where(row < L, 0.0, NEG_BIG), kr)
        kcat_ref[:, LANE:] = kr.astype(BF16)
        vaug_ref[:, LANE:] = _ones_lane_block(Lp)

    kcat_ref[:, :LANE] = kn_ref[...]
    qr = _rope_rot(qr_ref[...].astype(F32), c, s_up, s_dn)
    qcat_ref[:, :LANE] = qn_ref[...]
    qcat_ref[:, LANE:] = jnp.where(lane == MLA_ROPE, 1.0, qr).astype(BF16)
    vaug_ref[:, :LANE] = v_ref[...]

    def probs(row0, size, buf):
        s = lax.dot_general(qcat_ref[pl.ds(row0, size), :], kcat_ref[...], (((1,), (1,)), ((), ())),
                            preferred_element_type=F32)
        p_ref[buf, :size, :] = jnp.exp2(s - jnp.max(s, axis=-1, keepdims=True)).astype(BF16)

    def consume(row0, size, buf):
        o_ref[pl.ds(row0, size), :] = _pv(p_ref[buf, :size, :], vaug_ref[...]).astype(BF16)

    _pipeline_row_blocks(Lp, probs, consume)


def _mla_attention(q, kv, wide2, tabs, *, B, Lp, L, kr_col):
    H = MLA_HEADS
    krb = kr_col // LANE
    blk = lambda f: pl.BlockSpec((Lp, LANE), f)
    tab = blk(lambda b, h: (0, 0))
    cat = pltpu.VMEM((Lp, 2 * LANE), BF16)
    return pl.pallas_call(
        functools.partial(_mla_kernel, L=L), grid=(B, H),
        in_specs=[blk(lambda b, h: (b, h)), blk(lambda b, h: (b, H + h)), blk(lambda b, h: (b, h)),
                  blk(lambda b, h: (b, krb)), blk(lambda b, h: (b, H + h)), tab, tab, tab],
        out_specs=blk(lambda b, h: (b, h)),
        out_shape=jax.ShapeDtypeStruct((B * Lp, H * LANE), BF16),
        scratch_shapes=[cat, cat, cat, pltpu.VMEM((3, min(ATT_SUB, Lp), Lp), BF16)],
        compiler_params=_cp("parallel", "arbitrary"), name="latent_attn")(
            q, q, kv, wide2, kv, *tabs)


def _rope_tables(Lp):
    half = MLA_ROPE // 2
    inv = ROPE_THETA ** (-jnp.arange(half, dtype=F32) / half)
    ang = jnp.arange(Lp, dtype=F32)[:, None] * inv[None, :]
    cos, sin, z = jnp.cos(ang), jnp.sin(ang), jnp.zeros((Lp, half), F32)
    return (jnp.concatenate([cos, cos, z, z], axis=1),
            jnp.concatenate([z, sin, z, z], axis=1),
            jnp.concatenate([-sin, z, z, z], axis=1))


def _hy_pre_kernel(x0_ref, x1_ref, v_ref, w0_ref, w1_ref, wv_ref, b0_ref, b1_ref, bv_ref,
                   z_ref, x0c_ref, *, L):
    Lp = x0_ref.shape[0]
    row = lax.broadcasted_iota(I32, (Lp, 1), 0)
    valid = row < L

    def short_conv(u_ref, w_ref, b_ref):
        u = jnp.where(valid, u_ref[...].astype(F32), 0.0)
        w = w_ref[...]
        return (pltpu.roll(u, 1, 0) * w[0:1] + u * w[1:2] + pltpu.roll(u, Lp - 1, 0) * w[2:3]
                + b_ref[...])

    x0c_ref[...] = short_conv(x0_ref, w0_ref, b0_ref).astype(BF16)
    z = short_conv(x1_ref, w1_ref, b1_ref) * short_conv(v_ref, wv_ref, bv_ref)
    z_ref[...] = jnp.where(valid, z, 0.0).astype(BF16)


def _hy_pre(wide, short_w, short_b, *, B, Lp, L, col0):
    assert Lp > L
    nc = HY_WIDTH // LANE
    c0 = col0 // LANE
    u = lambda part: pl.BlockSpec((Lp, LANE), lambda b, c: (b, c0 + part * nc + c))
    w = lambda part: pl.BlockSpec((3, LANE), lambda b, c: (0, part * nc + c))
    bb = lambda part: pl.BlockSpec((1, LANE), lambda b, c: (0, part * nc + c))
    out = pl.BlockSpec((Lp, LANE), lambda b, c: (b, c))
    shp = jax.ShapeDtypeStruct((B * Lp, HY_WIDTH), BF16)
    sb = short_b.reshape(1, -1)
    return pl.pallas_call(
        functools.partial(_hy_pre_kernel, L=L), grid=(B, nc),
        in_specs=[u(0), u(1), u(2), w(0), w(1), w(2), bb(0), bb(1), bb(2)],
        out_specs=[out, out], out_shape=[shp, shp],
        compiler_params=_cp("parallel", "parallel"), name="hyena_gate")(
            wide, wide, wide, short_w, short_w, short_w, sb, sb, sb)


def _hy_fwd_kernel(fc_ref, fs_ref, z_ref, kr_ref, ki_ref, yr_ref, yi_ref):
    z = z_ref[...]
    zr = jnp.dot(fc_ref[...], z, preferred_element_type=F32)
    zi = jnp.dot(fs_ref[...], z, preferred_element_type=F32)
    kr, ki = kr_ref[...], ki_ref[...]
    yr_ref[...] = (zr * kr - zi * ki).astype(BF16)
    yi_ref[...] = (zr * ki + zi * kr).astype(BF16)


def _hy_fwd(fcat, z, kr, ki, *, B, Lp):
    C = z.shape[1]
    tf, tn = _tile(Lp, 576), _tile(C, 512, LANE)
    nf = Lp // tf
    fspec = lambda half: pl.BlockSpec((tf, Lp), lambda i, b, j: (i, half))
    kspec = pl.BlockSpec((tf, tn), lambda i, b, j: (i, j))
    out = pl.BlockSpec((tf, tn), lambda i, b, j: (b * nf + i, j))
    shp = jax.ShapeDtypeStruct((B * Lp, C), BF16)
    return pl.pallas_call(
        _hy_fwd_kernel, grid=(nf, B, C // tn),
        in_specs=[fspec(0), fspec(1), pl.BlockSpec((Lp, tn), lambda i, b, j: (b, j)), kspec, kspec],
        out_specs=[out, out], out_shape=[shp, shp],
        compiler_params=_cp("parallel", "parallel", "arbitrary"), name="hyena_dft")(
            fcat, fcat, z, kr, ki)


def _hy_inv_kernel(fc_ref, fs_ref, yr_ref, yi_ref, o_ref):
    acc = jnp.dot(fc_ref[...], yr_ref[...], preferred_element_type=F32)
    acc += jnp.dot(fs_ref[...], yi_ref[...], preferred_element_type=F32)
    o_ref[...] = acc


def _hy_inv(fcat, yr, yi, *, B, Lp):
    C = yr.shape[1]
    tf, tn = _tile(Lp, 576), _tile(C, 512, LANE)
    nf = Lp // tf
    fspec = lambda half: pl.BlockSpec((tf, Lp), lambda i, b, j: (i, half))
    yspec = pl.BlockSpec((Lp, tn), lambda i, b, j: (b, j))
    return pl.pallas_call(
        _hy_inv_kernel, grid=(nf, B, C // tn),
        in_specs=[fspec(0), fspec(1), yspec, yspec],
        out_specs=pl.BlockSpec((tf, tn), lambda i, b, j: (b * nf + i, j)),
        out_shape=jax.ShapeDtypeStruct((B * Lp, C), F32),
        compiler_params=_cp("parallel", "parallel", "arbitrary"), name="hyena_idft")(
            fcat, fcat, yr, yi)


def _hy_post_kernel(y_ref, z_ref, x0_ref, d_ref, g_ref, o_ref):
    y = y_ref[...] + z_ref[...].astype(F32) * d_ref[...]
    o = x0_ref[...].astype(F32) * y
    o = o * lax.rsqrt(jnp.mean(o * o, axis=-1, keepdims=True) + RMS_EPS) * g_ref[...]
    o_ref[...] = o.astype(BF16)


def _hy_post(y, z, x0c, d, g):
    R, C = y.shape
    tm = _tile(R, 512)
    row = pl.BlockSpec((tm, C), lambda i: (i, 0))
    vec = pl.BlockSpec((1, C), lambda i: (0, 0))
    return pl.pallas_call(
        _hy_post_kernel, grid=(R // tm,), in_specs=[row, row, row, vec, vec], out_specs=row,
        out_shape=jax.ShapeDtypeStruct((R, C), BF16),
        compiler_params=_cp("parallel"), name="hyena_out")(y, z, x0c, d.reshape(1, C), g.reshape(1, C))


def _dft_table(Lp):
    N = 2 * Lp - 1
    blk = 64
    f = jnp.arange(Lp, dtype=I32)[:, None]
    th = 2.0 * math.pi / N
    a0 = ((f * jnp.arange(0, Lp, blk, dtype=I32)[None, :]) % N).astype(F32) * th
    a1 = ((f * jnp.arange(blk, dtype=I32)[None, :]) % N).astype(F32) * th
    c0, s0, c1, s1 = jnp.cos(a0)[:, :, None], jnp.sin(a0)[:, :, None], jnp.cos(a1)[:, None, :], jnp.sin(a1)[:, None, :]
    fc = (c0 * c1 - s0 * s1).reshape(Lp, Lp)
    fs = (s0 * c1 + c0 * s1).reshape(Lp, Lp)
    return jnp.concatenate([fc, -fs], axis=1).astype(BF16)


def _hyena_taps(L, Lp, w1, b1, w2, b2, w3, b3, w4, freq):
    hp = lax.Precision.HIGHEST
    t = jnp.linspace(0.0, 1.0, L, dtype=F32)[:, None]
    w = (2.0 * math.pi / L) * jnp.arange(L, dtype=F32)
    bands = jnp.linspace(1e-4, HY_BANDS - 1, HY_BANDS, dtype=F32)
    ang = w[:, None] * bands[None, :]
    z = jnp.concatenate([t, jnp.cos(ang), -jnp.sin(ang)], axis=-1)
    h = jnp.sin(freq * (jnp.dot(z, w1, precision=hp) + b1))
    h = jnp.sin(freq * (jnp.dot(h, w2, precision=hp) + b2))
    h = jnp.sin(freq * (jnp.dot(h, w3, precision=hp) + b3))
    h = jnp.dot(h, w4, precision=hp).reshape(L, 2, HY_WIDTH)
    deltas = jnp.linspace(math.log(HY_DECAY_TARGET) / HY_DECAY_LONG, math.log(HY_DECAY_TARGET) / HY_DECAY_SHORT,
                          HY_WIDTH, dtype=F32)
    h = h * jnp.exp(-t * jnp.abs(deltas))[:, None, :]
    hf = h[:, 0]
    hb = h[:, 1].at[0].set(0.0)
    norm = jnp.sum(jnp.abs(hf), axis=0, keepdims=True) + jnp.sum(jnp.abs(hb), axis=0, keepdims=True)
    pad = ((0, Lp - L), (0, 0))
    return jnp.pad(hf / norm, pad), jnp.pad(hb / norm, pad)


def _hyena_spectrum(fcat, hf, hb, Lp):
    N = 2 * Lp - 1
    tm, tn = _tile(Lp, 576), _tile(HY_WIDTH, 512, LANE)
    kr = _mm(fcat, (hf + hb).astype(BF16), tm=tm, tn=tn, out_dtype=F32, k=Lp, x_kblock=0, name="hyena_spec_re")
    ki = _mm(fcat, (hf - hb).astype(BF16), tm=tm, tn=tn, out_dtype=F32, k=Lp, x_kblock=1, name="hyena_spec_im")
    wf = jnp.where(jnp.arange(Lp)[:, None] == 0, 1.0 / N, 2.0 / N).astype(F32)
    return kr * wf, ki * wf


def _router_kernel(x_ref, wh_ref, wl_ref, b_ref, info_ref, cnt_ref, carry_ref):
    tm = x_ref.shape[0]

    @pl.when(pl.program_id(0) == 0)
    def _():
        carry_ref[...] = jnp.zeros_like(carry_ref)

    x = x_ref[...]
    xh = x.astype(BF16)
    xl = (x - xh.astype(F32)).astype(BF16)
    wh, wl = wh_ref[...], wl_ref[...]
    logits = (jnp.dot(xh, wh, preferred_element_type=F32) + jnp.dot(xl, wh, preferred_element_type=F32)
              + jnp.dot(xh, wl, preferred_element_type=F32)) + b_ref[...]
    lane = lax.broadcasted_iota(I32, (tm, LANE), 1)
    vals, hots = [], []
    cur = logits
    for _ in range(TOP_K):
        m = jnp.max(cur, axis=-1, keepdims=True)
        idx = jnp.min(jnp.where(cur == m, lane, LANE), axis=-1, keepdims=True)
        hot = lane == idx
        vals.append(m)
        hots.append(hot)
        cur = jnp.where(hot, NEG_BIG, cur)
    es = [jnp.exp(v - vals[0]) for v in vals]
    den = es[0] + es[1] + es[2] + es[3]
    multihot = (hots[0] | hots[1] | hots[2] | hots[3]).astype(F32)
    r_i = lax.broadcasted_iota(I32, (tm, tm), 0)
    c_i = lax.broadcasted_iota(I32, (tm, tm), 1)
    tri = (c_i < r_i).astype(BF16)
    before = jnp.dot(tri, multihot.astype(BF16), preferred_element_type=F32) + carry_ref[...]
    info = jnp.zeros((tm, LANE), F32)
    for kk in range(TOP_K):
        e_k = jnp.sum(jnp.where(hots[kk], lane, 0), axis=-1, keepdims=True).astype(F32)
        r_k = jnp.sum(jnp.where(hots[kk], before, 0.0), axis=-1, keepdims=True)
        info = jnp.where(lane == kk, e_k, info)
        info = jnp.where(lane == TOP_K + kk, es[kk] / den, info)
        info = jnp.where(lane == 2 * TOP_K + kk, r_k, info)
    info_ref[...] = info
    carry_ref[...] += jnp.sum(multihot, axis=0, keepdims=True)
    cnt_ref[...] = carry_ref[...]


def _router(x, wh, wl, b):
    R, D = x.shape
    tm = _tile(R, 512)
    return pl.pallas_call(
        _router_kernel, grid=(R // tm,),
        in_specs=[pl.BlockSpec((tm, D), lambda i: (i, 0)), pl.BlockSpec((D, LANE), lambda i: (0, 0)),
                  pl.BlockSpec((D, LANE), lambda i: (0, 0)), pl.BlockSpec((1, LANE), lambda i: (0, 0))],
        out_specs=[pl.BlockSpec((tm, LANE), lambda i: (i, 0)), pl.BlockSpec((1, LANE), lambda i: (0, 0))],
        out_shape=[jax.ShapeDtypeStruct((R, LANE), F32), jax.ShapeDtypeStruct((1, LANE), F32)],
        scratch_shapes=[pltpu.VMEM((1, LANE), F32)],
        compiler_params=_cp("arbitrary"), name="router")(x, wh, wl, b)


def _row_copy(src_hbm, dst_ref, src_row, dst_row, sem):
    return pltpu.make_async_copy(src_hbm.at[pl.ds(src_row, 1)], dst_ref.at[pl.ds(dst_row, 1)], sem)


def _experts_kernel(be_ref, nb_ref, tok_ref, tok_next_ref, x_hbm, wg_ref, bg_ref, wu_ref, bu_ref, wd_ref,
                    bd_ref, o_ref, xbuf_ref, sem):
    j = pl.program_id(0)
    nb = nb_ref[0]
    G, half = xbuf_ref.shape[1], xbuf_ref.shape[2]
    cur = lax.rem(j, 2)

    def gather(idx_ref, buf):
        def start(r, c):
            _row_copy(x_hbm, xbuf_ref.at[buf], idx_ref[0, 0, r], r, sem.at[buf]).start()
            return c

        lax.fori_loop(0, G, start, 0, unroll=8)

    @pl.when(j == 0)
    def _():
        gather(tok_ref, 0)

    @pl.when(j + 1 < nb)
    def _():
        gather(tok_next_ref, 1 - cur)

    @pl.when(j < nb)
    def _():
        pltpu.make_async_copy(x_hbm.at[pl.ds(0, G)], xbuf_ref.at[cur], sem.at[cur]).wait()
        xw = xbuf_ref[cur]
        xlo, xhi = _unpack_lo(xw).astype(BF16), _unpack_hi(xw).astype(BF16)

        def proj(w_ref, b_ref):
            return (jnp.dot(xlo, w_ref[0, :half], preferred_element_type=F32)
                    + jnp.dot(xhi, w_ref[0, half:], preferred_element_type=F32) + b_ref[0])

        g = jnp.minimum(proj(wg_ref, bg_ref), SWIGLU_LIMIT)
        u = jnp.clip(proj(wu_ref, bu_ref), -SWIGLU_LIMIT, SWIGLU_LIMIT)
        hid = ((u + 1.0) * g * jax.nn.sigmoid(SWIGLU_ALPHA * g)).astype(BF16)
        o = jnp.dot(hid, wd_ref[0], preferred_element_type=F32) + bd_ref[0]
        o_ref[...] = _pack_pair(o[:, :half], o[:, half:])

    @pl.when(j >= nb)
    def _():
        o_ref[...] = jnp.zeros_like(o_ref)


def _experts(block_expert, nb, tok_of_slot, xp, wg, bg, wu, bu, wd, bd, *, n_blocks):
    G = MOE_BLOCK
    E, D, F = wg.shape
    W = xp.shape[1]
    wsp = lambda shape: pl.BlockSpec((1,) + shape, lambda j, be, nb: (be[j], 0, 0))
    toks = tok_of_slot.reshape(n_blocks, 1, G)
    tsp = lambda d: pl.BlockSpec((1, 1, G), lambda j, be, nb: (jnp.minimum(j + d, n_blocks - 1), 0, 0),
                                 memory_space=pltpu.SMEM)
    return pl.pallas_call(
        _experts_kernel,
        grid_spec=pltpu.PrefetchScalarGridSpec(
            num_scalar_prefetch=2, grid=(n_blocks,),
            in_specs=[tsp(0), tsp(1), pl.BlockSpec(memory_space=pl.ANY),
                      wsp((D, F)), wsp((1, F)), wsp((D, F)), wsp((1, F)), wsp((F, D)), wsp((1, D))],
            out_specs=pl.BlockSpec((G, W), lambda j, be, nb: (j, 0)),
            scratch_shapes=[pltpu.VMEM((2, G, W), U32), pltpu.SemaphoreType.DMA((2,))]),
        out_shape=jax.ShapeDtypeStruct((n_blocks * G, W), U32),
        compiler_params=_cp("arbitrary"), name="moe_experts")(
            block_expert, nb, toks, toks, xp, wg, bg.reshape(E, 1, F), wu, bu.reshape(E, 1, F), wd,
            bd.reshape(E, 1, D))


def _combine_kernel(slot_ref, slot_next_ref, w_ref, h_ref, g_ref, b_ref, os_hbm, of_ref, ob_ref, buf_ref, sem,
                    *, alpha):
    i = pl.program_id(0)
    tm = h_ref.shape[0]
    cur = lax.rem(i, 2)

    def gather(idx_ref, buf):
        def start(r, c):
            for kk in range(TOP_K):
                _row_copy(os_hbm, buf_ref.at[buf, kk], idx_ref[0, 0, r * TOP_K + kk], r, sem.at[buf]).start()
            return c

        lax.fori_loop(0, tm, start, 0, unroll=4)

    @pl.when(i == 0)
    def _():
        gather(slot_ref, 0)

    @pl.when(i + 1 < pl.num_programs(0))
    def _():
        gather(slot_next_ref, 1 - cur)

    for kk in range(TOP_K):
        pltpu.make_async_copy(os_hbm.at[pl.ds(0, tm)], buf_ref.at[cur, kk], sem.at[cur]).wait()
    w = w_ref[...]
    lo = jnp.zeros((tm, buf_ref.shape[3]), F32)
    hi = jnp.zeros((tm, buf_ref.shape[3]), F32)
    for kk in range(TOP_K):
        word = buf_ref[cur, kk]
        wk = w[:, TOP_K + kk:TOP_K + kk + 1]
        lo += wk * _unpack_lo(word)
        hi += wk * _unpack_hi(word)
    y = _ln_math(alpha * h_ref[...] + jnp.concatenate([lo, hi], axis=1), g_ref[...], b_ref[...])
    of_ref[...] = y
    ob_ref[...] = y.astype(BF16)


def _combine(slot, info, h, g, b, os_, alpha):
    R, D = h.shape
    tm = _tile(R, 128)
    nblk = R // tm
    row = pl.BlockSpec((tm, D), lambda i: (i, 0))
    vec = pl.BlockSpec((1, D), lambda i: (0, 0))
    slots = slot.reshape(nblk, 1, tm * TOP_K)
    ssp = lambda d: pl.BlockSpec((1, 1, tm * TOP_K), lambda i: (jnp.minimum(i + d, nblk - 1), 0, 0),
                                 memory_space=pltpu.SMEM)
    return pl.pallas_call(
        functools.partial(_combine_kernel, alpha=alpha), grid=(nblk,),
        in_specs=[ssp(0), ssp(1), pl.BlockSpec((tm, LANE), lambda i: (i, 0)), row, vec, vec,
                  pl.BlockSpec(memory_space=pl.ANY)],
        out_specs=[row, row],
        out_shape=[jax.ShapeDtypeStruct((R, D), F32), jax.ShapeDtypeStruct((R, D), BF16)],
        scratch_shapes=[pltpu.VMEM((2, TOP_K, tm, D // 2), U32), pltpu.SemaphoreType.DMA((2,))],
        compiler_params=_cp("arbitrary"), name="moe_combine")(
            slots, slots, info, h, g.reshape(1, D), b.reshape(1, D), os_)


def _moe(hf, hp, p, alpha):
    R = hf.shape[0]
    G = MOE_BLOCK
    info, counts = _router(hf, p["wr_hi"], p["wr_lo"], p["br"])
    expert = info[:, :TOP_K].astype(I32)
    rank = info[:, 2 * TOP_K:3 * TOP_K].astype(I32)
    counts = counts[0, :N_EXPERTS].astype(I32)
    padded = ((counts + G - 1) // G) * G
    ends = jnp.cumsum(padded)
    slot = (ends - padded)[expert] + rank
    n_blocks = -(-(R * TOP_K) // G) + N_EXPERTS
    nb = (ends[-1] // G).astype(I32).reshape(1)
    starts = jnp.arange(n_blocks, dtype=I32)[:, None] * G
    block_expert = jnp.minimum(jnp.sum((ends[None, :] <= starts).astype(I32), axis=1), N_EXPERTS - 1)
    tok = jnp.broadcast_to(jnp.arange(R, dtype=I32)[:, None], (R, TOP_K))
    tok_of_slot = jnp.zeros((n_blocks * G,), I32).at[slot.reshape(-1)].set(tok.reshape(-1))
    os_ = _experts(block_expert, nb, tok_of_slot, hp, p["wg"], p["bg"], p["wu"], p["bu"], p["wd"], p["bd"],
                   n_blocks=n_blocks)
    return _combine(slot, info, hf, p["ln2_g"], p["ln2_b"], os_, alpha)


def _prep_layer(l, lam_init, w):
    D = D_MODEL
    daw, hw = DA_HEADS * DA_V_DIM, HY_WIDTH
    sizes = (2 * DA_HEADS * DA_QK_DIM, 2 * DA_HEADS * DA_QK_DIM, daw, MLA_Q_RANK, MLA_KV_RANK, MLA_ROPE, 3 * hw)
    offs = [0]
    for s in sizes:
        offs.append(offs[-1] + s)
    w_in = w["w_in"][l]
    part = lambda i: w_in[:, offs[i]:offs[i + 1]]
    log2e = math.log2(math.e)
    qa = part(0) * (DA_QK_DIM ** -0.5 * log2e)
    p = {}
    p["w_in1"] = jnp.concatenate([part(3), part(6), qa, part(1), part(2)], axis=1).astype(BF16)
    p["w_in2"] = jnp.concatenate([part(4), part(5), jnp.zeros((D, LANE - MLA_ROPE), F32)], axis=1).astype(BF16)
    H = MLA_HEADS
    wq = (w["w_uq"][l] * ((MLA_NOPE + MLA_ROPE) ** -0.5 * log2e)).reshape(MLA_Q_RANK, H, MLA_NOPE + MLA_ROPE)
    wq_rope = jnp.pad(wq[:, :, MLA_NOPE:], ((0, 0), (0, 0), (0, LANE - MLA_ROPE)))
    p["w_uq"] = jnp.concatenate([wq[:, :, :MLA_NOPE].reshape(MLA_Q_RANK, H * MLA_NOPE),
                                 wq_rope.reshape(MLA_Q_RANK, H * LANE)], axis=1).astype(BF16)
    wkv = w["w_ukv"][l].reshape(MLA_KV_RANK, H, MLA_NOPE + MLA_V)
    p["w_ukv"] = jnp.concatenate([wkv[:, :, :MLA_NOPE].reshape(MLA_KV_RANK, H * MLA_NOPE),
                                  wkv[:, :, MLA_NOPE:].reshape(MLA_KV_RANK, H * MLA_V)], axis=1).astype(BF16)
    p["q_norm_g"], p["kv_norm_g"] = w["mla_q_norm_g"][l], w["mla_kv_norm_g"][l]
    w_o = w["w_o"][l].astype(BF16)
    p["w_o"] = (w_o[:daw], w_o[daw:daw + H * MLA_V], w_o[daw + H * MLA_V:])
    p["mla_out_g"] = w["mla_out_g"][l]
    lv = w["da_lambda"][l]
    lam = jnp.exp(jnp.sum(lv[0] * lv[1])) - jnp.exp(jnp.sum(lv[2] * lv[3])) + lam_init
    slopes = jnp.exp2(-8.0 * jnp.arange(1, DA_HEADS + 1, dtype=F32) / DA_HEADS)
    p["da_scal"] = jnp.concatenate([slopes * log2e, lam.reshape(1), jnp.zeros((3,), F32)]).astype(F32)
    p["da_gain"] = w["da_norm_g"][l] * (1.0 - lam_init)
    for name in ("hy_short_w", "hy_short_b", "hy_d", "hy_out_g", "hy_ffn_w1", "hy_ffn_b1", "hy_ffn_w2",
                 "hy_ffn_b2", "hy_ffn_w3", "hy_ffn_b3", "hy_ffn_w4", "hy_freq", "ln1_g", "ln1_b", "ln2_g", "ln2_b"):
        p[name] = w[name][l]
    wr = jnp.pad(w["w_router"][l], ((0, 0), (0, LANE - N_EXPERTS)))
    p["wr_hi"] = wr.astype(BF16)
    p["wr_lo"] = (wr - p["wr_hi"].astype(F32)).astype(BF16)
    p["br"] = jnp.pad(w["b_router"][l], (0, LANE - N_EXPERTS), constant_values=NEG_BIG).reshape(1, LANE)
    p["wg"], p["wu"], p["wd"] = w["w_gate"][l].astype(BF16), w["w_up"][l].astype(BF16), w["w_down"][l].astype(BF16)
    p["bg"], p["bu"], p["bd"] = w["b_gate"][l], w["b_up"][l], w["b_down"][l]
    return p


def _layer(hf, hb, p, geo, consts, alpha):
    B, Lp, L = geo
    R = B * Lp
    fcat, rope_tabs = consts
    hw, daw = HY_WIDTH, DA_HEADS * DA_V_DIM
    tm = _tile(R, 1100)
    n1 = p["w_in1"].shape[1]
    wide1 = _mm(hb, p["w_in1"], tm=tm, tn=_tile(n1, 512, LANE), out_dtype=BF16, name="w_in_main")
    wide2 = _mm(hb, p["w_in2"], tm=tm, tn=p["w_in2"].shape[1], out_dtype=BF16, name="w_in_kv")
    hy_col, da_col = MLA_Q_RANK, MLA_Q_RANK + 3 * hw

    oa = _da_attention(wide1, p["da_scal"], p["da_gain"], B=B, Lp=Lp, L=L, col0=da_col)

    tm2 = _tile(R, 512)
    q = _rms_mm(wide1, p["q_norm_g"], p["w_uq"], tm=tm2, tn=_tile(p["w_uq"].shape[1], 1536, LANE), name="mla_q_up")
    kv = _rms_mm(wide2, p["kv_norm_g"], p["w_ukv"], tm=tm2, tn=_tile(p["w_ukv"].shape[1], 1536, LANE), name="mla_kv_up")
    ob = _mla_attention(q, kv, wide2, rope_tabs, B=B, Lp=Lp, L=L, kr_col=MLA_KV_RANK)

    z, x0c = _hy_pre(wide1, p["hy_short_w"], p["hy_short_b"], B=B, Lp=Lp, L=L, col0=hy_col)
    hf_taps, hb_taps = _hyena_taps(L, Lp, p["hy_ffn_w1"], p["hy_ffn_b1"], p["hy_ffn_w2"], p["hy_ffn_b2"],
                                   p["hy_ffn_w3"], p["hy_ffn_b3"], p["hy_ffn_w4"], p["hy_freq"])
    kr, ki = _hyena_spectrum(fcat, hf_taps, hb_taps, Lp)
    yr, yi = _hy_fwd(fcat, z, kr, ki, B=B, Lp=Lp)
    y = _hy_inv(fcat, yr, yi, B=B, Lp=Lp)
    oc = _hy_post(y, z, x0c, p["hy_d"], p["hy_out_g"])

    mix = _wo(oa, ob, oc, p["mla_out_g"], *p["w_o"])
    h1f, h1p = _ln_mix(hf, mix, p["ln1_g"], p["ln1_b"], alpha)
    return _moe(h1f, h1p, p, alpha)


def kernel(x_prompt, x_sample, meta_tokens, emb_ln_g, emb_ln_b, w_in, w_o, da_lambda, da_norm_g, mla_q_norm_g, mla_kv_norm_g, w_uq, w_ukv, mla_out_g, hy_short_w, hy_short_b, hy_ffn_w1, hy_ffn_b1, hy_ffn_w2, hy_ffn_b2, hy_ffn_w3, hy_ffn_b3, hy_ffn_w4, hy_freq, hy_d, hy_out_g, ln1_g, ln1_b, ln2_g, ln2_b, w_router, b_router, w_gate, b_gate, w_up, b_up, w_down, b_down):
    w = dict(w_in=w_in, w_o=w_o, da_lambda=da_lambda, da_norm_g=da_norm_g, mla_q_norm_g=mla_q_norm_g,
             mla_kv_norm_g=mla_kv_norm_g, w_uq=w_uq, w_ukv=w_ukv, mla_out_g=mla_out_g, hy_short_w=hy_short_w,
             hy_short_b=hy_short_b, hy_ffn_w1=hy_ffn_w1, hy_ffn_b1=hy_ffn_b1, hy_ffn_w2=hy_ffn_w2,
             hy_ffn_b2=hy_ffn_b2, hy_ffn_w3=hy_ffn_w3, hy_ffn_b3=hy_ffn_b3, hy_ffn_w4=hy_ffn_w4, hy_freq=hy_freq,
             hy_d=hy_d, hy_out_g=hy_out_g, ln1_g=ln1_g, ln1_b=ln1_b, ln2_g=ln2_g, ln2_b=ln2_b, w_router=w_router,
             b_router=b_router, w_gate=w_gate, b_gate=b_gate, w_up=w_up, b_up=b_up, w_down=w_down, b_down=b_down)
    depth = w_in.shape[0]
    alpha = (2 * depth) ** 0.25
    D = x_prompt.shape[-1]
    layers = [_prep_layer(l, 0.8 - 0.6 * math.exp(-0.3 * l), w) for l in range(depth)]

    outs = []
    for x in (x_prompt, x_sample):
        B, L0, _ = x.shape
        L = L0 + N_META
        Lp = -(-L // LANE) * LANE
        meta = jnp.broadcast_to(meta_tokens[None].astype(x.dtype), (B, N_META, D))
        rows = jnp.concatenate([meta, x, jnp.zeros((B, Lp - L, D), x.dtype)], axis=1).reshape(B * Lp, D)
        hf, hb = _ln_embed(rows, emb_ln_g, emb_ln_b)
        consts = (_dft_table(Lp), _rope_tables(Lp))
        for p in layers:
            hf, hb = _layer(hf, hb, p, (B, Lp, L), consts, alpha)
        outs.append(hf.reshape(B, Lp, D)[:, N_META:L])
    return tuple(outs)
```

```python
import functools
import math

import jax
import jax.numpy as jnp
from jax import lax
from jax.experimental import pallas as pl
from jax.experimental.pallas import tpu as pltpu

F32, BF16, I32, U32 = jnp.float32, jnp.bfloat16, jnp.int32, jnp.uint32

D_MODEL = 4096
DEPTH = 2
N_META = 16
LN_EPS = 1e-5
RMS_EPS = 1e-6
DA_HEADS = 12
DA_QK_DIM = 64
DA_V_DIM = 128
MLA_HEADS = 12
MLA_Q_RANK = 1024
MLA_KV_RANK = 512
MLA_NOPE = 128
MLA_ROPE = 64
MLA_V = 128
ROPE_THETA = 10000.0
HY_WIDTH = 1024
HY_BANDS = 16
HY_DECAY_TARGET = 1e-2
HY_DECAY_SHORT = 0.3
HY_DECAY_LONG = 1.5
N_EXPERTS = 32
TOP_K = 4
D_FF = 512
SWIGLU_LIMIT = 7.0
SWIGLU_ALPHA = 1.702

LANE = 128
VMEM_LIMIT = 56 * 1024 * 1024
MOE_BLOCK = 512
NEG_BIG = -1e30
FAR_POS = 1e9


def _cp(*sem):
    return pltpu.CompilerParams(dimension_semantics=sem, vmem_limit_bytes=VMEM_LIMIT)


def _tile(n, target, mult=16):
    best = None
    for d in range(mult, min(n, target) + 1, mult):
        if n % d == 0:
            best = d
    assert best is not None, (n, target, mult)
    return best


def _ln_math(x, g, b):
    mu = jnp.mean(x, axis=-1, keepdims=True)
    xc = x - mu
    var = jnp.mean(xc * xc, axis=-1, keepdims=True)
    return xc * lax.rsqrt(var + LN_EPS) * g + b


def _pack_pair(lo, hi):
    lo_b = lax.bitcast_convert_type(lo.astype(BF16).astype(F32), U32)
    hi_b = lax.bitcast_convert_type(hi.astype(BF16).astype(F32), U32)
    return (lo_b >> 16) | (hi_b & jnp.uint32(0xFFFF0000))


def _unpack_lo(w):
    return lax.bitcast_convert_type(w << 16, F32)


def _unpack_hi(w):
    return lax.bitcast_convert_type(w & jnp.uint32(0xFFFF0000), F32)


def _ln_embed_kernel(x_ref, g_ref, b_ref, of_ref, ob_ref):
    y = _ln_math(x_ref[...], g_ref[...], b_ref[...])
    of_ref[...] = y
    ob_ref[...] = y.astype(BF16)


def _ln_embed(x, g, b):
    R, D = x.shape
    tm = _tile(R, 256)
    row = pl.BlockSpec((tm, D), lambda i: (i, 0))
    vec = pl.BlockSpec((1, D), lambda i: (0, 0))
    return pl.pallas_call(
        _ln_embed_kernel, grid=(R // tm,), in_specs=[row, vec, vec], out_specs=[row, row],
        out_shape=[jax.ShapeDtypeStruct((R, D), F32), jax.ShapeDtypeStruct((R, D), BF16)],
        compiler_params=_cp("parallel"), name="ln_embed")(x, g.reshape(1, D), b.reshape(1, D))


def _ln_mix_kernel(h_ref, m_ref, g_ref, b_ref, of_ref, op_ref, *, alpha):
    y = _ln_math(alpha * h_ref[...] + m_ref[...], g_ref[...], b_ref[...])
    of_ref[...] = y
    half = y.shape[1] // 2
    op_ref[...] = _pack_pair(y[:, :half], y[:, half:])


def _ln_mix(h, mix, g, b, alpha):
    R, D = h.shape
    tm = _tile(R, 256)
    row = pl.BlockSpec((tm, D), lambda i: (i, 0))
    vec = pl.BlockSpec((1, D), lambda i: (0, 0))
    return pl.pallas_call(
        functools.partial(_ln_mix_kernel, alpha=alpha), grid=(R // tm,),
        in_specs=[row, row, vec, vec],
        out_specs=[row, pl.BlockSpec((tm, D // 2), lambda i: (i, 0))],
        out_shape=[jax.ShapeDtypeStruct((R, D), F32), jax.ShapeDtypeStruct((R, D // 2), U32)],
        compiler_params=_cp("parallel"), name="ln_mix")(h, mix, g.reshape(1, D), b.reshape(1, D))


def _mm_kernel(x_ref, w_ref, o_ref):
    o_ref[...] = jnp.dot(x_ref[...], w_ref[...], preferred_element_type=F32).astype(o_ref.dtype)


def _mm(x, w, *, tm, tn, out_dtype, k=None, x_kblock=0, name="mm"):
    M = x.shape[0]
    K, N = w.shape
    assert (k or x.shape[1]) == K and M % tm == 0 and N % tn == 0
    return pl.pallas_call(
        _mm_kernel, grid=(M // tm, N // tn),
        in_specs=[pl.BlockSpec((tm, K), lambda i, j: (i, x_kblock)),
                  pl.BlockSpec((K, tn), lambda i, j: (0, j))],
        out_specs=pl.BlockSpec((tm, tn), lambda i, j: (i, j)),
        out_shape=jax.ShapeDtypeStruct((M, N), out_dtype),
        compiler_params=_cp("parallel", "arbitrary"), name=name)(x, w)


def _rms_mm_kernel(x_ref, g_ref, w_ref, o_ref):
    x = x_ref[...].astype(F32)
    inv = lax.rsqrt(jnp.mean(x * x, axis=-1, keepdims=True) + RMS_EPS)
    xn = (x * inv * g_ref[...]).astype(BF16)
    o_ref[...] = jnp.dot(xn, w_ref[...], preferred_element_type=F32).astype(o_ref.dtype)


def _rms_mm(x, g, w, *, tm, tn, name):
    M = x.shape[0]
    K, N = w.shape
    assert M % tm == 0 and N % tn == 0
    return pl.pallas_call(
        _rms_mm_kernel, grid=(M // tm, N // tn),
        in_specs=[pl.BlockSpec((tm, K), lambda i, j: (i, 0)),
                  pl.BlockSpec((1, K), lambda i, j: (0, 0)),
                  pl.BlockSpec((K, tn), lambda i, j: (0, j))],
        out_specs=pl.BlockSpec((tm, tn), lambda i, j: (i, j)),
        out_shape=jax.ShapeDtypeStruct((M, N), BF16),
        compiler_params=_cp("parallel", "arbitrary"), name=name)(x, g.reshape(1, K), w)


def _wo_kernel(oa_ref, ob_ref, oc_ref, gb_ref, w1_ref, w2_ref, w3_ref, o_ref):
    ob = ob_ref[...].astype(F32)
    inv = lax.rsqrt(jnp.mean(ob * ob, axis=-1, keepdims=True) + RMS_EPS)
    obn = (ob * inv * gb_ref[...]).astype(BF16)
    acc = jnp.dot(oa_ref[...], w1_ref[...], preferred_element_type=F32)
    acc += jnp.dot(obn, w2_ref[...], preferred_element_type=F32)
    acc += jnp.dot(oc_ref[...], w3_ref[...], preferred_element_type=F32)
    o_ref[...] = acc


def _wo(oa, ob, oc, gb, w1, w2, w3):
    R = oa.shape[0]
    N = w1.shape[1]
    tm, tn = _tile(R, 512), _tile(N, 1024, LANE)
    xs = lambda a: pl.BlockSpec((tm, a.shape[1]), lambda j, i: (i, 0))
    ws = lambda a: pl.BlockSpec((a.shape[0], tn), lambda j, i: (0, j))
    return pl.pallas_call(
        _wo_kernel, grid=(N // tn, R // tm),
        in_specs=[xs(oa), xs(ob), xs(oc), pl.BlockSpec((1, ob.shape[1]), lambda j, i: (0, 0)),
                  ws(w1), ws(w2), ws(w3)],
        out_specs=pl.BlockSpec((tm, tn), lambda j, i: (i, j)),
        out_shape=jax.ShapeDtypeStruct((R, N), F32),
        compiler_params=_cp("parallel", "arbitrary"), name="w_o")(
            oa, ob, oc, gb.reshape(1, -1), w1, w2, w3)


ATT_SUB = 256


def _for_row_blocks(Lp, step):
    n_full = Lp // ATT_SUB
    if n_full:
        def body(r, c):
            step(pl.multiple_of(r * ATT_SUB, ATT_SUB), ATT_SUB)
            return c

        lax.fori_loop(0, n_full, body, 0, unroll=2)
    if Lp % ATT_SUB:
        step(n_full * ATT_SUB, Lp % ATT_SUB)


def _pipeline_row_blocks(Lp, probs, consume):
    S = ATT_SUB
    n_full, tail = Lp // S, Lp % S
    if n_full < 2 or n_full % 2:
        def step(row0, size):
            probs(row0, size, 0)
            consume(row0, size, 0)

        _for_row_blocks(Lp, step)
        return
    probs(0, S, 0)

    def body(i, c):
        r0 = pl.multiple_of(i * (2 * S), 2 * S)
        r1 = pl.multiple_of(r0 + S, S)
        probs(r1, S, 1)
        consume(r0, S, 0)
        probs(pl.multiple_of(r0 + 2 * S, 2 * S), S, 0)
        consume(r1, S, 1)
        return c

    lax.fori_loop(0, n_full // 2 - 1, body, 0)
    r0 = (n_full - 2) * S
    probs(r0 + S, S, 1)
    consume(r0, S, 0)
    if tail:
        probs(n_full * S, tail, 2)
    consume(r0 + S, S, 1)
    if tail:
        consume(n_full * S, tail, 2)


def _ones_lane_block(rows):
    lane = lax.broadcasted_iota(I32, (rows, LANE), 1)
    return jnp.where(lane == 0, 1.0, 0.0).astype(BF16)


def _pv(p, vaug):
    oa = jnp.dot(p, vaug, preferred_element_type=F32)
    return oa[:, :LANE] / oa[:, LANE:LANE + 1]


def _da_kernel(sc_ref, q_ref, k_ref, v_ref, g_ref, o_ref, vaug_ref, p_ref, *, L, n_heads):
    h = pl.program_id(1)
    Lp = k_ref.shape[0]
    slope, lam = sc_ref[h], sc_ref[n_heads]
    vaug_ref[:, :LANE] = v_ref[...]
    vaug_ref[:, LANE:] = _ones_lane_block(Lp)
    col = lax.broadcasted_iota(I32, (1, Lp), 1)
    kpos = jnp.where(col < L, col.astype(F32), FAR_POS) * slope
    dn = (((1,), (1,)), ((), ()))

    def probs(row0, size, buf):
        q = q_ref[pl.ds(row0, size), :]
        lane = lax.broadcasted_iota(I32, (size, LANE), 1)
        zero = jnp.zeros_like(q)
        k = k_ref[...]
        qpos = (row0 + lax.broadcasted_iota(I32, (size, 1), 0)).astype(F32) * slope
        bias = jnp.abs(qpos - kpos)
        for m, qm in enumerate((jnp.where(lane < DA_QK_DIM, q, zero), jnp.where(lane >= DA_QK_DIM, q, zero))):
            s = lax.dot_general(qm, k, dn, preferred_element_type=F32) - bias
            p_ref[buf, m, :size, :] = jnp.exp2(s - jnp.max(s, axis=-1, keepdims=True)).astype(BF16)

    def consume(row0, size, buf):
        vaug = vaug_ref[...]
        o = _pv(p_ref[buf, 0, :size, :], vaug) - lam * _pv(p_ref[buf, 1, :size, :], vaug)
        o = o * lax.rsqrt(jnp.mean(o * o, axis=-1, keepdims=True) + RMS_EPS) * g_ref[...]
        o_ref[pl.ds(row0, size), :] = o.astype(BF16)

    _pipeline_row_blocks(Lp, probs, consume)


def _da_attention(wide, scal, gain, *, B, Lp, L, col0):
    H = DA_HEADS
    c0 = col0 // LANE
    blk = lambda part: pl.BlockSpec((Lp, LANE), lambda b, h: (b, c0 + part * H + h))
    return pl.pallas_call(
        functools.partial(_da_kernel, L=L, n_heads=H), grid=(B, H),
        in_specs=[pl.BlockSpec(memory_space=pltpu.SMEM), blk(0), blk(1), blk(2),
                  pl.BlockSpec((1, LANE), lambda b, h: (0, 0))],
        out_specs=pl.BlockSpec((Lp, LANE), lambda b, h: (b, h)),
        out_shape=jax.ShapeDtypeStruct((B * Lp, H * LANE), BF16),
        scratch_shapes=[pltpu.VMEM((Lp, 2 * LANE), BF16), pltpu.VMEM((3, 2, min(ATT_SUB, Lp), Lp), BF16)],
        compiler_params=_cp("parallel", "parallel"), name="diff_attn")(
            scal, wide, wide, wide, gain.reshape(1, LANE))


def _rope_rot(x, c, s_up, s_dn):
    half = MLA_ROPE // 2
    return x * c + pltpu.roll(x, half, 1) * s_up + pltpu.roll(x, LANE - half, 1) * s_dn


def _mla_kernel(qn_ref, qr_ref, kn_ref, kr_ref, v_ref, c_ref, up_ref, dn_ref, o_ref,
                qcat_ref, kcat_ref, vaug_ref, p_ref, *, L):
    Lp = kn_ref.shape[0]
    c, s_up, s_dn = c_ref[...], up_ref[...], dn_ref[...]
    lane = lax.broadcasted_iota(I32, (Lp, LANE), 1)

    @pl.when(pl.program_id(1) == 0)
    def _():
        row = lax.broadcasted_iota(I32, (Lp, 1), 0)
        kr = _rope_rot(kr_ref[...].astype(F32), c, s_up, s_dn)
        kr = jnp.where(lane == MLA_ROPE, jnp.where(row < L, 0.0, NEG_BIG), kr)
        kcat_ref[:, LANE:] = kr.astype(BF16)
        vaug_ref[:, LANE:] = _ones_lane_block(Lp)

    kcat_ref[:, :LANE] = kn_ref[...]
    qr = _rope_rot(qr_ref[...].astype(F32), c, s_up, s_dn)
    qcat_ref[:, :LANE] = qn_ref[...]
    qcat_ref[:, LANE:] = jnp.where(lane == MLA_ROPE, 1.0, qr).astype(BF16)
    vaug_ref[:, :LANE] = v_ref[...]

    def probs(row0, size, buf):
        s = lax.dot_general(qcat_ref[pl.ds(row0, size), :], kcat_ref[...], (((1,), (1,)), ((), ())),
                            preferred_element_type=F32)
        p_ref[buf, :size, :] = jnp.exp2(s - jnp.max(s, axis=-1, keepdims=True)).astype(BF16)

    def consume(row0, size, buf):
        o_ref[pl.ds(row0, size), :] = _pv(p_ref[buf, :size, :], vaug_ref[...]).astype(BF16)

    _pipeline_row_blocks(Lp, probs, consume)


def _mla_attention(q, kv, wide2, tabs, *, B, Lp, L, kr_col):
    H = MLA_HEADS
    krb = kr_col // LANE
    blk = lambda f: pl.BlockSpec((Lp, LANE), f)
    tab = blk(lambda b, h: (0, 0))
    cat = pltpu.VMEM((Lp, 2 * LANE), BF16)
    return pl.pallas_call(
        functools.partial(_mla_kernel, L=L), grid=(B, H),
        in_specs=[blk(lambda b, h: (b, h)), blk(lambda b, h: (b, H + h)), blk(lambda b, h: (b, h)),
                  blk(lambda b, h: (b, krb)), blk(lambda b, h: (b, H + h)), tab, tab, tab],
        out_specs=blk(lambda b, h: (b, h)),
        out_shape=jax.ShapeDtypeStruct((B * Lp, H * LANE), BF16),
        scratch_shapes=[cat, cat, cat, pltpu.VMEM((3, min(ATT_SUB, Lp), Lp), BF16)],
        compiler_params=_cp("parallel", "arbitrary"), name="latent_attn")(
            q, q, kv, wide2, kv, *tabs)


def _rope_tables(Lp):
    half = MLA_ROPE // 2
    inv = ROPE_THETA ** (-jnp.arange(half, dtype=F32) / half)
    ang = jnp.arange(Lp, dtype=F32)[:, None] * inv[None, :]
    cos, sin, z = jnp.cos(ang), jnp.sin(ang), jnp.zeros((Lp, half), F32)
    return (jnp.concatenate([cos, cos, z, z], axis=1),
            jnp.concatenate([z, sin, z, z], axis=1),
            jnp.concatenate([-sin, z, z, z], axis=1))


def _hy_pre_kernel(x0_ref, x1_ref, v_ref, w0_ref, w1_ref, wv_ref, b0_ref, b1_ref, bv_ref,
                   z_ref, x0c_ref, *, L):
    Lp = x0_ref.shape[0]
    row = lax.broadcasted_iota(I32, (Lp, 1), 0)
    valid = row < L

    def short_conv(u_ref, w_ref, b_ref):
        u = jnp.where(valid, u_ref[...].astype(F32), 0.0)
        w = w_ref[...]
        return (pltpu.roll(u, 1, 0) * w[0:1] + u * w[1:2] + pltpu.roll(u, Lp - 1, 0) * w[2:3]
                + b_ref[...])

    x0c_ref[...] = short_conv(x0_ref, w0_ref, b0_ref).astype(BF16)
    z = short_conv(x1_ref, w1_ref, b1_ref) * short_conv(v_ref, wv_ref, bv_ref)
    z_ref[...] = jnp.where(valid, z, 0.0).astype(BF16)


def _hy_pre(wide, short_w, short_b, *, B, Lp, L, col0):
    assert Lp > L
    nc = HY_WIDTH // LANE
    c0 = col0 // LANE
    u = lambda part: pl.BlockSpec((Lp, LANE), lambda b, c: (b, c0 + part * nc + c))
    w = lambda part: pl.BlockSpec((3, LANE), lambda b, c: (0, part * nc + c))
    bb = lambda part: pl.BlockSpec((1, LANE), lambda b, c: (0, part * nc + c))
    out = pl.BlockSpec((Lp, LANE), lambda b, c: (b, c))
    shp = jax.ShapeDtypeStruct((B * Lp, HY_WIDTH), BF16)
    sb = short_b.reshape(1, -1)
    return pl.pallas_call(
        functools.partial(_hy_pre_kernel, L=L), grid=(B, nc),
        in_specs=[u(0), u(1), u(2), w(0), w(1), w(2), bb(0), bb(1), bb(2)],
        out_specs=[out, out], out_shape=[shp, shp],
        compiler_params=_cp("parallel", "parallel"), name="hyena_gate")(
            wide, wide, wide, short_w, short_w, short_w, sb, sb, sb)


def _hy_fwd_kernel(fc_ref, fs_ref, z_ref, kr_ref, ki_ref, yr_ref, yi_ref):
    z = z_ref[...]
    zr = jnp.dot(fc_ref[...], z, preferred_element_type=F32)
    zi = jnp.dot(fs_ref[...], z, preferred_element_type=F32)
    kr, ki = kr_ref[...], ki_ref[...]
    yr_ref[...] = (zr * kr - zi * ki).astype(BF16)
    yi_ref[...] = (zr * ki + zi * kr).astype(BF16)


def _hy_fwd(fcat, z, kr, ki, *, B, Lp):
    C = z.shape[1]
    tf, tn = _tile(Lp, 576), _tile(C, 512, LANE)
    nf = Lp // tf
    fspec = lambda half: pl.BlockSpec((tf, Lp), lambda i, b, j: (i, half))
    kspec = pl.BlockSpec((tf, tn), lambda i, b, j: (i, j))
    out = pl.BlockSpec((tf, tn), lambda i, b, j: (b * nf + i, j))
    shp = jax.ShapeDtypeStruct((B * Lp, C), BF16)
    return pl.pallas_call(
        _hy_fwd_kernel, grid=(nf, B, C // tn),
        in_specs=[fspec(0), fspec(1), pl.BlockSpec((Lp, tn), lambda i, b, j: (b, j)), kspec, kspec],
        out_specs=[out, out], out_shape=[shp, shp],
        compiler_params=_cp("parallel", "parallel", "arbitrary"), name="hyena_dft")(
            fcat, fcat, z, kr, ki)


def _hy_inv_kernel(fc_ref, fs_ref, yr_ref, yi_ref, o_ref):
    acc = jnp.dot(fc_ref[...], yr_ref[...], preferred_element_type=F32)
    acc += jnp.dot(fs_ref[...], yi_ref[...], preferred_element_type=F32)
    o_ref[...] = acc


def _hy_inv(fcat, yr, yi, *, B, Lp):
    C = yr.shape[1]
    tf, tn = _tile(Lp, 576), _tile(C, 512, LANE)
    nf = Lp // tf
    fspec = lambda half: pl.BlockSpec((tf, Lp), lambda i, b, j: (i, half))
    yspec = pl.BlockSpec((Lp, tn), lambda i, b, j: (b, j))
    return pl.pallas_call(
        _hy_inv_kernel, grid=(nf, B, C // tn),
        in_specs=[fspec(0), fspec(1), yspec, yspec],
        out_specs=pl.BlockSpec((tf, tn), lambda i, b, j: (b * nf + i, j)),
        out_shape=jax.ShapeDtypeStruct((B * Lp, C), F32),
        compiler_params=_cp("parallel", "parallel", "arbitrary"), name="hyena_idft")(
            fcat, fcat, yr, yi)


def _hy_post_kernel(y_ref, z_ref, x0_ref, d_ref, g_ref, o_ref):
    y = y_ref[...] + z_ref[...].astype(F32) * d_ref[...]
    o = x0_ref[...].astype(F32) * y
    o = o * lax.rsqrt(jnp.mean(o * o, axis=-1, keepdims=True) + RMS_EPS) * g_ref[...]
    o_ref[...] = o.astype(BF16)


def _hy_post(y, z, x0c, d, g):
    R, C = y.shape
    tm = _tile(R, 512)
    row = pl.BlockSpec((tm, C), lambda i: (i, 0))
    vec = pl.BlockSpec((1, C), lambda i: (0, 0))
    return pl.pallas_call(
        _hy_post_kernel, grid=(R // tm,), in_specs=[row, row, row, vec, vec], out_specs=row,
        out_shape=jax.ShapeDtypeStruct((R, C), BF16),
        compiler_params=_cp("parallel"), name="hyena_out")(y, z, x0c, d.reshape(1, C), g.reshape(1, C))


def _dft_table(Lp):
    N = 2 * Lp - 1
    blk = 64
    f = jnp.arange(Lp, dtype=I32)[:, None]
    th = 2.0 * math.pi / N
    a0 = ((f * jnp.arange(0, Lp, blk, dtype=I32)[None, :]) % N).astype(F32) * th
    a1 = ((f * jnp.arange(blk, dtype=I32)[None, :]) % N).astype(F32) * th
    c0, s0, c1, s1 = jnp.cos(a0)[:, :, None], jnp.sin(a0)[:, :, None], jnp.cos(a1)[:, None, :], jnp.sin(a1)[:, None, :]
    fc = (c0 * c1 - s0 * s1).reshape(Lp, Lp)
    fs = (s0 * c1 + c0 * s1).reshape(Lp, Lp)
    return jnp.concatenate([fc, -fs], axis=1).astype(BF16)


def _hyena_taps(L, Lp, w1, b1, w2, b2, w3, b3, w4, freq):
    hp = lax.Precision.HIGHEST
    t = jnp.linspace(0.0, 1.0, L, dtype=F32)[:, None]
    w = (2.0 * math.pi / L) * jnp.arange(L, dtype=F32)
    bands = jnp.linspace(1e-4, HY_BANDS - 1, HY_BANDS, dtype=F32)
    ang = w[:, None] * bands[None, :]
    z = jnp.concatenate([t, jnp.cos(ang), -jnp.sin(ang)], axis=-1)
    h = jnp.sin(freq * (jnp.dot(z, w1, precision=hp) + b1))
    h = jnp.sin(freq * (jnp.dot(h, w2, precision=hp) + b2))
    h = jnp.sin(freq * (jnp.dot(h, w3, precision=hp) + b3))
    h = jnp.dot(h, w4, precision=hp).reshape(L, 2, HY_WIDTH)
    deltas = jnp.linspace(math.log(HY_DECAY_TARGET) / HY_DECAY_LONG, math.log(HY_DECAY_TARGET) / HY_DECAY_SHORT,
                          HY_WIDTH, dtype=F32)
    h = h * jnp.exp(-t * jnp.abs(deltas))[:, None, :]
    hf = h[:, 0]
    hb = h[:, 1].at[0].set(0.0)
    norm = jnp.sum(jnp.abs(hf), axis=0, keepdims=True) + jnp.sum(jnp.abs(hb), axis=0, keepdims=True)
    pad = ((0, Lp - L), (0, 0))
    return jnp.pad(hf / norm, pad), jnp.pad(hb / norm, pad)


def _hyena_spectrum(fcat, hf, hb, Lp):
    N = 2 * Lp - 1
    tm, tn = _tile(Lp, 576), _tile(HY_WIDTH, 512, LANE)
    kr = _mm(fcat, (hf + hb).astype(BF16), tm=tm, tn=tn, out_dtype=F32, k=Lp, x_kblock=0, name="hyena_spec_re")
    ki = _mm(fcat, (hf - hb).astype(BF16), tm=tm, tn=tn, out_dtype=F32, k=Lp, x_kblock=1, name="hyena_spec_im")
    wf = jnp.where(jnp.arange(Lp)[:, None] == 0, 1.0 / N, 2.0 / N).astype(F32)
    return kr * wf, ki * wf


def _router_kernel(x_ref, wh_ref, wl_ref, b_ref, info_ref, cnt_ref, carry_ref):
    tm = x_ref.shape[0]

    @pl.when(pl.program_id(0) == 0)
    def _():
        carry_ref[...] = jnp.zeros_like(carry_ref)

    x = x_ref[...]
    xh = x.astype(BF16)
    xl = (x - xh.astype(F32)).astype(BF16)
    wh, wl = wh_ref[...], wl_ref[...]
    logits = (jnp.dot(xh, wh, preferred_element_type=F32) + jnp.dot(xl, wh, preferred_element_type=F32)
              + jnp.dot(xh, wl, preferred_element_type=F32)) + b_ref[...]
    lane = lax.broadcasted_iota(I32, (tm, LANE), 1)
    vals, hots = [], []
    cur = logits
    for _ in range(TOP_K):
        m = jnp.max(cur, axis=-1, keepdims=True)
        idx = jnp.min(jnp.where(cur == m, lane, LANE), axis=-1, keepdims=True)
        hot = lane == idx
        vals.append(m)
        hots.append(hot)
        cur = jnp.where(hot, NEG_BIG, cur)
    es = [jnp.exp(v - vals[0]) for v in vals]
    den = es[0] + es[1] + es[2] + es[3]
    multihot = (hots[0] | hots[1] | hots[2] | hots[3]).astype(F32)
    r_i = lax.broadcasted_iota(I32, (tm, tm), 0)
    c_i = lax.broadcasted_iota(I32, (tm, tm), 1)
    tri = (c_i < r_i).astype(BF16)
    before = jnp.dot(tri, multihot.astype(BF16), preferred_element_type=F32) + carry_ref[...]
    info = jnp.zeros((tm, LANE), F32)
    for kk in range(TOP_K):
        e_k = jnp.sum(jnp.where(hots[kk], lane, 0), axis=-1, keepdims=True).astype(F32)
        r_k = jnp.sum(jnp.where(hots[kk], before, 0.0), axis=-1, keepdims=True)
        info = jnp.where(lane == kk, e_k, info)
        info = jnp.where(lane == TOP_K + kk, es[kk] / den, info)
        info = jnp.where(lane == 2 * TOP_K + kk, r_k, info)
    info_ref[...] = info
    carry_ref[...] += jnp.sum(multihot, axis=0, keepdims=True)
    cnt_ref[...] = carry_ref[...]


def _router(x, wh, wl, b):
    R, D = x.shape
    tm = _tile(R, 512)
    return pl.pallas_call(
        _router_kernel, grid=(R // tm,),
        in_specs=[pl.BlockSpec((tm, D), lambda i: (i, 0)), pl.BlockSpec((D, LANE), lambda i: (0, 0)),
                  pl.BlockSpec((D, LANE), lambda i: (0, 0)), pl.BlockSpec((1, LANE), lambda i: (0, 0))],
        out_specs=[pl.BlockSpec((tm, LANE), lambda i: (i, 0)), pl.BlockSpec((1, LANE), lambda i: (0, 0))],
        out_shape=[jax.ShapeDtypeStruct((R, LANE), F32), jax.ShapeDtypeStruct((1, LANE), F32)],
        scratch_shapes=[pltpu.VMEM((1, LANE), F32)],
        compiler_params=_cp("arbitrary"), name="router")(x, wh, wl, b)


def _row_copy(src_hbm, dst_ref, src_row, dst_row, sem):
    return pltpu.make_async_copy(src_hbm.at[pl.ds(src_row, 1)], dst_ref.at[pl.ds(dst_row, 1)], sem)


def _experts_kernel(be_ref, nb_ref, tok_ref, tok_next_ref, x_hbm, wg_ref, bg_ref, wu_ref, bu_ref, wd_ref,
                    bd_ref, o_ref, xbuf_ref, sem):
    j = pl.program_id(0)
    nb = nb_ref[0]
    G, half = xbuf_ref.shape[1], xbuf_ref.shape[2]
    cur = lax.rem(j, 2)

    def gather(idx_ref, buf):
        def start(r, c):
            _row_copy(x_hbm, xbuf_ref.at[buf], idx_ref[0, 0, r], r, sem.at[buf]).start()
            return c

        lax.fori_loop(0, G, start, 0, unroll=16)

    @pl.when(j == 0)
    def _():
        gather(tok_ref, 0)

    @pl.when(j + 1 < nb)
    def _():
        gather(tok_next_ref, 1 - cur)

    @pl.when(j < nb)
    def _():
        pltpu.make_async_copy(x_hbm.at[pl.ds(0, G)], xbuf_ref.at[cur], sem.at[cur]).wait()
        xw = xbuf_ref[cur]
        xlo, xhi = _unpack_lo(xw).astype(BF16), _unpack_hi(xw).astype(BF16)

        def proj(w_ref, b_ref):
            return (jnp.dot(xlo, w_ref[0, :half], preferred_element_type=F32)
                    + jnp.dot(xhi, w_ref[0, half:], preferred_element_type=F32) + b_ref[0])

        g = jnp.minimum(proj(wg_ref, bg_ref), SWIGLU_LIMIT)
        u = jnp.clip(proj(wu_ref, bu_ref), -SWIGLU_LIMIT, SWIGLU_LIMIT)
        hid = ((u + 1.0) * g * jax.nn.sigmoid(SWIGLU_ALPHA * g)).astype(BF16)
        o = jnp.dot(hid, wd_ref[0], preferred_element_type=F32) + bd_ref[0]
        o_ref[...] = _pack_pair(o[:, :half], o[:, half:])

    @pl.when(j >= nb)
    def _():
        o_ref[...] = jnp.zeros_like(o_ref)


def _experts(block_expert, nb, tok_of_slot, xp, wg, bg, wu, bu, wd, bd, *, n_blocks):
    G = MOE_BLOCK
    E, D, F = wg.shape
    W = xp.shape[1]
    wsp = lambda shape: pl.BlockSpec((1,) + shape, lambda j, be, nb: (be[j], 0, 0))
    toks = tok_of_slot.reshape(n_blocks, 1, G)
    tsp = lambda d: pl.BlockSpec((1, 1, G), lambda j, be, nb: (jnp.minimum(j + d, n_blocks - 1), 0, 0),
                                 memory_space=pltpu.SMEM)
    return pl.pallas_call(
        _experts_kernel,
        grid_spec=pltpu.PrefetchScalarGridSpec(
            num_scalar_prefetch=2, grid=(n_blocks,),
            in_specs=[tsp(0), tsp(1), pl.BlockSpec(memory_space=pl.ANY),
                      wsp((D, F)), wsp((1, F)), wsp((D, F)), wsp((1, F)), wsp((F, D)), wsp((1, D))],
            out_specs=pl.BlockSpec((G, W), lambda j, be, nb: (j, 0)),
            scratch_shapes=[pltpu.VMEM((2, G, W), U32), pltpu.SemaphoreType.DMA((2,))]),
        out_shape=jax.ShapeDtypeStruct((n_blocks * G, W), U32),
        compiler_params=_cp("arbitrary"), name="moe_experts")(
            block_expert, nb, toks, toks, xp, wg, bg.reshape(E, 1, F), wu, bu.reshape(E, 1, F), wd,
            bd.reshape(E, 1, D))


def _combine_kernel(slot_ref, slot_next_ref, w_ref, h_ref, g_ref, b_ref, os_hbm, of_ref, ob_ref, buf_ref, sem,
                    *, alpha):
    i = pl.program_id(0)
    tm = h_ref.shape[0]
    cur = lax.rem(i, 2)

    def gather(idx_ref, buf):
        def start(r, c):
            for kk in range(TOP_K):
                _row_copy(os_hbm, buf_ref.at[buf, kk], idx_ref[0, 0, r * TOP_K + kk], r, sem.at[buf]).start()
            return c

        lax.fori_loop(0, tm, start, 0, unroll=4)

    @pl.when(i == 0)
    def _():
        gather(slot_ref, 0)

    @pl.when(i + 1 < pl.num_programs(0))
    def _():
        gather(slot_next_ref, 1 - cur)

    for kk in range(TOP_K):
        pltpu.make_async_copy(os_hbm.at[pl.ds(0, tm)], buf_ref.at[cur, kk], sem.at[cur]).wait()
    w = w_ref[...]
    lo = jnp.zeros((tm, buf_ref.shape[3]), F32)
    hi = jnp.zeros((tm, buf_ref.shape[3]), F32)
    for kk in range(TOP_K):
        word = buf_ref[cur, kk]
        wk = w[:, TOP_K + kk:TOP_K + kk + 1]
        lo += wk * _unpack_lo(word)
        hi += wk * _unpack_hi(word)
    y = _ln_math(alpha * h_ref[...] + jnp.concatenate([lo, hi], axis=1), g_ref[...], b_ref[...])
    of_ref[...] = y
    ob_ref[...] = y.astype(BF16)


def _combine(slot, info, h, g, b, os_, alpha):
    R, D = h.shape
    tm = _tile(R, 128)
    nblk = R // tm
    row = pl.BlockSpec((tm, D), lambda i: (i, 0))
    vec = pl.BlockSpec((1, D), lambda i: (0, 0))
    slots = slot.reshape(nblk, 1, tm * TOP_K)
    ssp = lambda d: pl.BlockSpec((1, 1, tm * TOP_K), lambda i: (jnp.minimum(i + d, nblk - 1), 0, 0),
                                 memory_space=pltpu.SMEM)
    return pl.pallas_call(
        functools.partial(_combine_kernel, alpha=alpha), grid=(nblk,),
        in_specs=[ssp(0), ssp(1), pl.BlockSpec((tm, LANE), lambda i: (i, 0)), row, vec, vec,
                  pl.BlockSpec(memory_space=pl.ANY)],
        out_specs=[row, row],
        out_shape=[jax.ShapeDtypeStruct((R, D), F32), jax.ShapeDtypeStruct((R, D), BF16)],
        scratch_shapes=[pltpu.VMEM((2, TOP_K, tm, D // 2), U32), pltpu.SemaphoreType.DMA((2,))],
        compiler_params=_cp("arbitrary"), name="moe_combine")(
            slots, slots, info, h, g.reshape(1, D), b.reshape(1, D), os_)


def _moe(hf, hp, p, alpha):
    R = hf.shape[0]
    G = MOE_BLOCK
    info, counts = _router(hf, p["wr_hi"], p["wr_lo"], p["br"])
    expert = info[:, :TOP_K].astype(I32)
    rank = info[:, 2 * TOP_K:3 * TOP_K].astype(I32)
    counts = counts[0, :N_EXPERTS].astype(I32)
    padded = ((counts + G - 1) // G) * G
    ends = jnp.cumsum(padded)
    slot = (ends - padded)[expert] + rank
    n_blocks = -(-(R * TOP_K) // G) + N_EXPERTS
    nb = (ends[-1] // G).astype(I32).reshape(1)
    starts = jnp.arange(n_blocks, dtype=I32)[:, None] * G
    block_expert = jnp.minimum(jnp.sum((ends[None, :] <= starts).astype(I32), axis=1), N_EXPERTS - 1)
    tok = jnp.broadcast_to(jnp.arange(R, dtype=I32)[:, None], (R, TOP_K))
    tok_of_slot = jnp.zeros((n_blocks * G,), I32).at[slot.reshape(-1)].set(tok.reshape(-1), unique_indices=True)
    os_ = _experts(block_expert, nb, tok_of_slot, hp, p["wg"], p["bg"], p["wu"], p["bu"], p["wd"], p["bd"],
                   n_blocks=n_blocks)
    return _combine(slot, info, hf, p["ln2_g"], p["ln2_b"], os_, alpha)


def _prep_layer(l, lam_init, w):
    D = D_MODEL
    daw, hw = DA_HEADS * DA_V_DIM, HY_WIDTH
    sizes = (2 * DA_HEADS * DA_QK_DIM, 2 * DA_HEADS * DA_QK_DIM, daw, MLA_Q_RANK, MLA_KV_RANK, MLA_ROPE, 3 * hw)
    offs = [0]
    for s in sizes:
        offs.append(offs[-1] + s)
    w_in = w["w_in"][l]
    part = lambda i: w_in[:, offs[i]:offs[i + 1]]
    log2e = math.log2(math.e)
    qa = part(0) * (DA_QK_DIM ** -0.5 * log2e)
    p = {}
    p["w_in1"] = jnp.concatenate([part(3), part(6), qa, part(1), part(2)], axis=1).astype(BF16)
    p["w_in2"] = jnp.concatenate([part(4), part(5), jnp.zeros((D, LANE - MLA_ROPE), F32)], axis=1).astype(BF16)
    H = MLA_HEADS
    wq = (w["w_uq"][l] * ((MLA_NOPE + MLA_ROPE) ** -0.5 * log2e)).reshape(MLA_Q_RANK, H, MLA_NOPE + MLA_ROPE)
    wq_rope = jnp.pad(wq[:, :, MLA_NOPE:], ((0, 0), (0, 0), (0, LANE - MLA_ROPE)))
    p["w_uq"] = jnp.concatenate([wq[:, :, :MLA_NOPE].reshape(MLA_Q_RANK, H * MLA_NOPE),
                                 wq_rope.reshape(MLA_Q_RANK, H * LANE)], axis=1).astype(BF16)
    wkv = w["w_ukv"][l].reshape(MLA_KV_RANK, H, MLA_NOPE + MLA_V)
    p["w_ukv"] = jnp.concatenate([wkv[:, :, :MLA_NOPE].reshape(MLA_KV_RANK, H * MLA_NOPE),
                                  wkv[:, :, MLA_NOPE:].reshape(MLA_KV_RANK, H * MLA_V)], axis=1).astype(BF16)
    p["q_norm_g"], p["kv_norm_g"] = w["mla_q_norm_g"][l], w["mla_kv_norm_g"][l]
    w_o = w["w_o"][l].astype(BF16)
    p["w_o"] = (w_o[:daw], w_o[daw:daw + H * MLA_V], w_o[daw + H * MLA_V:])
    p["mla_out_g"] = w["mla_out_g"][l]
    lv = w["da_lambda"][l]
    lam = jnp.exp(jnp.sum(lv[0] * lv[1])) - jnp.exp(jnp.sum(lv[2] * lv[3])) + lam_init
    slopes = jnp.exp2(-8.0 * jnp.arange(1, DA_HEADS + 1, dtype=F32) / DA_HEADS)
    p["da_scal"] = jnp.concatenate([slopes * log2e, lam.reshape(1), jnp.zeros((3,), F32)]).astype(F32)
    p["da_gain"] = w["da_norm_g"][l] * (1.0 - lam_init)
    for name in ("hy_short_w", "hy_short_b", "hy_d", "hy_out_g", "hy_ffn_w1", "hy_ffn_b1", "hy_ffn_w2",
                 "hy_ffn_b2", "hy_ffn_w3", "hy_ffn_b3", "hy_ffn_w4", "hy_freq", "ln1_g", "ln1_b", "ln2_g", "ln2_b"):
        p[name] = w[name][l]
    wr = jnp.pad(w["w_router"][l], ((0, 0), (0, LANE - N_EXPERTS)))
    p["wr_hi"] = wr.astype(BF16)
    p["wr_lo"] = (wr - p["wr_hi"].astype(F32)).astype(BF16)
    p["br"] = jnp.pad(w["b_router"][l], (0, LANE - N_EXPERTS), constant_values=NEG_BIG).reshape(1, LANE)
    p["wg"], p["wu"], p["wd"] = w["w_gate"][l].astype(BF16), w["w_up"][l].astype(BF16), w["w_down"][l].astype(BF16)
    p["bg"], p["bu"], p["bd"] = w["b_gate"][l], w["b_up"][l], w["b_down"][l]
    return p


def _layer(hf, hb, p, geo, consts, alpha):
    B, Lp, L = geo
    R = B * Lp
    fcat, rope_tabs = consts
    hw, daw = HY_WIDTH, DA_HEADS * DA_V_DIM
    tm = _tile(R, 1100)
    n1 = p["w_in1"].shape[1]
    wide1 = _mm(hb, p["w_in1"], tm=tm, tn=_tile(n1, 512, LANE), out_dtype=BF16, name="w_in_main")
    wide2 = _mm(hb, p["w_in2"], tm=tm, tn=p["w_in2"].shape[1], out_dtype=BF16, name="w_in_kv")
    hy_col, da_col = MLA_Q_RANK, MLA_Q_RANK + 3 * hw

    oa = _da_attention(wide1, p["da_scal"], p["da_gain"], B=B, Lp=Lp, L=L, col0=da_col)

    tm2 = _tile(R, 512)
    q = _rms_mm(wide1, p["q_norm_g"], p["w_uq"], tm=tm2, tn=_tile(p["w_uq"].shape[1], 1536, LANE), name="mla_q_up")
    kv = _rms_mm(wide2, p["kv_norm_g"], p["w_ukv"], tm=tm2, tn=_tile(p["w_ukv"].shape[1], 1536, LANE), name="mla_kv_up")
    ob = _mla_attention(q, kv, wide2, rope_tabs, B=B, Lp=Lp, L=L, kr_col=MLA_KV_RANK)

    z, x0c = _hy_pre(wide1, p["hy_short_w"], p["hy_short_b"], B=B, Lp=Lp, L=L, col0=hy_col)
    hf_taps, hb_taps = _hyena_taps(L, Lp, p["hy_ffn_w1"], p["hy_ffn_b1"], p["hy_ffn_w2"], p["hy_ffn_b2"],
                                   p["hy_ffn_w3"], p["hy_ffn_b3"], p["hy_ffn_w4"], p["hy_freq"])
    kr, ki = _hyena_spectrum(fcat, hf_taps, hb_taps, Lp)
    yr, yi = _hy_fwd(fcat, z, kr, ki, B=B, Lp=Lp)
    y = _hy_inv(fcat, yr, yi, B=B, Lp=Lp)
    oc = _hy_post(y, z, x0c, p["hy_d"], p["hy_out_g"])

    mix = _wo(oa, ob, oc, p["mla_out_g"], *p["w_o"])
    h1f, h1p = _ln_mix(hf, mix, p["ln1_g"], p["ln1_b"], alpha)
    return _moe(h1f, h1p, p, alpha)


def kernel(x_prompt, x_sample, meta_tokens, emb_ln_g, emb_ln_b, w_in, w_o, da_lambda, da_norm_g, mla_q_norm_g, mla_kv_norm_g, w_uq, w_ukv, mla_out_g, hy_short_w, hy_short_b, hy_ffn_w1, hy_ffn_b1, hy_ffn_w2, hy_ffn_b2, hy_ffn_w3, hy_ffn_b3, hy_ffn_w4, hy_freq, hy_d, hy_out_g, ln1_g, ln1_b, ln2_g, ln2_b, w_router, b_router, w_gate, b_gate, w_up, b_up, w_down, b_down):
    w = dict(w_in=w_in, w_o=w_o, da_lambda=da_lambda, da_norm_g=da_norm_g, mla_q_norm_g=mla_q_norm_g,
             mla_kv_norm_g=mla_kv_norm_g, w_uq=w_uq, w_ukv=w_ukv, mla_out_g=mla_out_g, hy_short_w=hy_short_w,
             hy_short_b=hy_short_b, hy_ffn_w1=hy_ffn_w1, hy_ffn_b1=hy_ffn_b1, hy_ffn_w2=hy_ffn_w2,
             hy_ffn_b2=hy_ffn_b2, hy_ffn_w3=hy_ffn_w3, hy_ffn_b3=hy_ffn_b3, hy_ffn_w4=hy_ffn_w4, hy_freq=hy_freq,
             hy_d=hy_d, hy_out_g=hy_out_g, ln1_g=ln1_g, ln1_b=ln1_b, ln2_g=ln2_g, ln2_b=ln2_b, w_router=w_router,
             b_router=b_router, w_gate=w_gate, b_gate=b_gate, w_up=w_up, b_up=b_up, w_down=w_down, b_down=b_down)
    depth = w_in.shape[0]
    alpha = (2 * depth) ** 0.25
    D = x_prompt.shape[-1]
    layers = [_prep_layer(l, 0.8 - 0.6 * math.exp(-0.3 * l), w) for l in range(depth)]

    outs = []
    for x in (x_prompt, x_sample):
        B, L0, _ = x.shape
        L = L0 + N_META
        Lp = -(-L // LANE) * LANE
        meta = jnp.broadcast_to(meta_tokens[None].astype(x.dtype), (B, N_META, D))
        rows = jnp.concatenate([meta, x, jnp.zeros((B, Lp - L, D), x.dtype)], axis=1).reshape(B * Lp, D)
        hf, hb = _ln_embed(rows, emb_ln_g, emb_ln_b)
        consts = (_dft_table(Lp), _rope_tables(Lp))
        for p in layers:
            hf, hb = _layer(hf, hb, p, (B, Lp, L), consts, alpha)
        outs.append(hf.reshape(B, Lp, D)[:, N_META:L])
    return tuple(outs)
```
